```python
import math
import jax
import jax.numpy as jnp
from jax import lax
import numpy as np

D_MODEL = 1024
BATCH = 8
SEQ = 4096
DEPTH = 4

CTX_LEN = 256
GRID_W = 64
EPS = 1e-6

MIX_WIDTH = D_MODEL
POOL_WIDTH = MIX_WIDTH // 4
GDN_WIDTH = MIX_WIDTH // 2
NA_WIDTH = MIX_WIDTH - POOL_WIDTH - GDN_WIDTH

POOL_WINDOWS = (2, 4, 8, 16)
POOL_GROUPS = len(POOL_WINDOWS)
POOL_GROUP_DIM = POOL_WIDTH // POOL_GROUPS

GDN_HEAD_DIM = 128
GDN_HEADS = GDN_WIDTH // GDN_HEAD_DIM
GDN_CHUNK = 64
SHORT_CONV = 5
ROPE_THETA = 10000.0

NA_HEAD_DIM = 64
NA_HEADS = NA_WIDTH // NA_HEAD_DIM
NA_WIN_ROWS = 8
NA_WIN_COLS = 16
NA_QB = 16
NA_KB_COLS = NA_QB + NA_WIN_COLS

D_FF = 2816
FFN_CONV = 3

POOL_END = POOL_WIDTH
GQKV_END = POOL_END + 3 * GDN_WIDTH
GZ_END = GQKV_END + GDN_WIDTH
GAB_END = GZ_END + 4 * GDN_HEADS
N_IN = GAB_END + 3 * NA_WIDTH

kernel_name = "hybrid_pool_gdn_natten_dit_trunk"


def rmsnorm(x, w):
    xf = x.astype(jnp.float32)
    y = xf * lax.rsqrt(jnp.mean(xf * xf, axis=-1, keepdims=True) + EPS)
    return (y * w.astype(jnp.float32)).astype(x.dtype)


def l2norm(x):
    return x * lax.rsqrt(jnp.sum(x * x, axis=-1, keepdims=True) + EPS)


def centred_dwconv(x, w):
    k = w.shape[0]
    return lax.conv_general_dilated(
        x, w[:, None, :].astype(x.dtype), window_strides=(1,),
        padding=[(k // 2, k // 2)], dimension_numbers=("NWC", "WIO", "NWC"),
        feature_group_count=x.shape[-1])


def merge_heads(o):
    b, h, l, d = o.shape
    return o.transpose(0, 2, 1, 3).reshape(b, l, h * d)


def multiscale_pool(v, w_pool, scale):
    L = v.shape[1]
    vf = v.astype(jnp.float32)
    csum = jnp.pad(jnp.cumsum(vf, axis=1), ((0, 0), (1, 0), (0, 0)))
    t = jnp.arange(L)
    outs = []
    for g, win in enumerate(POOL_WINDOWS):
        lo = jnp.clip(t - win // 2, 0, L)
        hi = jnp.clip(t + win - win // 2, 0, L)
        sl = slice(g * POOL_GROUP_DIM, (g + 1) * POOL_GROUP_DIM)
        seg = csum[:, :, sl]
        mean = (seg[:, hi] - seg[:, lo]) / (hi - lo).astype(jnp.float32)[None, :, None]
        outs.append(mean - vf[:, :, sl])
    pooled = jnp.stack(outs, axis=2).astype(v.dtype)
    y = jnp.einsum("blgc,gcd->blgd", pooled, w_pool)
    return y.reshape(v.shape) * scale


def axial_rope(x):
    L, dim = x.shape[2], x.shape[-1]
    half = dim // 2
    nf = half // 2
    t = jnp.arange(L)
    freqs = ROPE_THETA ** (-jnp.arange(nf, dtype=jnp.float32) / nf)

    def rot(xp, pos):
        ang = pos.astype(jnp.float32)[:, None] * freqs
        cos, sin = jnp.cos(ang), jnp.sin(ang)
        x1, x2 = xp[..., :nf], xp[..., nf:]
        return jnp.concatenate([x1 * cos - x2 * sin, x1 * sin + x2 * cos], axis=-1)

    return jnp.concatenate([rot(x[..., :half], t // GRID_W), rot(x[..., half:], t % GRID_W)], axis=-1)


def gdn_prepare(p_qkv, p_ab, conv_w, a_log, dt_bias, rope):
    B, L, _ = p_qkv.shape
    qkv = jax.nn.silu(centred_dwconv(p_qkv, conv_w)).astype(jnp.float32)
    q, k, v = jnp.split(qkv, 3, axis=-1)

    def heads(t):
        return t.reshape(B, L, GDN_HEADS, GDN_HEAD_DIM).transpose(0, 2, 1, 3)

    q, k, v = l2norm(heads(q)), l2norm(heads(k)), heads(v)
    if rope:
        q, k = axial_rope(q), axial_rope(k)
    ab = p_ab.astype(jnp.float32).reshape(B, L, 2, 2, GDN_HEADS)
    a, b = ab[:, :, 0], ab[:, :, 1]
    g = -jnp.exp(a_log.astype(jnp.float32)) * jax.nn.softplus(a + dt_bias.astype(jnp.float32))
    beta = jax.nn.sigmoid(b)
    return q, k, v, g.transpose(2, 0, 3, 1), beta.transpose(2, 0, 3, 1)


def gdn_chunked(q, k, v, g, beta, state):
    B, H, L, dk = q.shape
    dv = v.shape[-1]
    n = L // GDN_CHUNK

    def ch(t):
        return t.reshape(B, H, n, GDN_CHUNK, *t.shape[3:])

    q, k, v, g, beta = ch(q) * dk ** -0.5, ch(k), ch(v), ch(g), ch(beta)
    gc = jnp.cumsum(g, axis=-1)
    incl = np.tril(np.ones((GDN_CHUNK, GDN_CHUNK), dtype=bool))
    decay = jnp.where(incl, jnp.exp(jnp.where(incl, gc[..., :, None] - gc[..., None, :], 0.0)), 0.0)
    kk = jnp.einsum("bhncd,bhnsd->bhncs", k * beta[..., None], k) * decay
    a_mat = jnp.tril(kk, -1) + jnp.eye(GDN_CHUNK, dtype=kk.dtype)
    rhs = jnp.concatenate([v * beta[..., None], k * (beta * jnp.exp(gc))[..., None]], axis=-1)
    uw = lax.linalg.triangular_solve(a_mat, rhs, left_side=True, lower=True, unit_diagonal=True)
    u, w = uw[..., :dv], uw[..., dv:]
    intra = jnp.einsum("bhncd,bhnsd->bhncs", q, k) * decay
    q_dec = q * jnp.exp(gc)[..., None]
    k_dec = k * jnp.exp(gc[..., -1:] - gc)[..., None]
    g_last = jnp.exp(gc[..., -1])

    def step(S, xs):
        u_n, w_n, intra_n, qd_n, kd_n, gl_n = xs
        v_new = u_n - jnp.einsum("bhck,bhkv->bhcv", w_n, S)
        o_n = jnp.einsum("bhck,bhkv->bhcv", qd_n, S) + jnp.einsum("bhcs,bhsv->bhcv", intra_n, v_new)
        S = S * gl_n[..., None, None] + jnp.einsum("bhck,bhcv->bhkv", kd_n, v_new)
        return S, o_n

    xs = tuple(jnp.moveaxis(t, 2, 0) for t in (u, w, intra, q_dec, k_dec, g_last))
    state, o = lax.scan(step, state, xs)
    return jnp.moveaxis(o, 0, 2).reshape(B, H, L, dv), state


def gdn_bidirectional(ctx_in, lat_in):
    qc, kc, vc, gc, bc = ctx_in
    ql, kl, vl, gl, bl = lat_in
    B, H, _, dk = qc.shape
    s0 = jnp.zeros((B, H, dk, vc.shape[-1]), jnp.float32)

    def flip(t):
        return jnp.flip(t, axis=2)

    oc_f, sc_f = gdn_chunked(qc, kc, vc, gc[0], bc[0], s0)
    ol_f, _ = gdn_chunked(ql, kl, vl, gl[0], bl[0], sc_f)
    oc_b, sc_b = gdn_chunked(flip(qc), flip(kc), flip(vc), flip(gc[1]), flip(bc[1]), s0)
    ol_b, _ = gdn_chunked(flip(ql), flip(kl), flip(vl), flip(gl[1]), flip(bl[1]), sc_b)
    return oc_f + flip(oc_b), ol_f + flip(ol_b)


def gdn_output(o, z, norm_w):
    B, H, L, dv = o.shape
    o = o.transpose(0, 2, 1, 3)
    on = o * lax.rsqrt(jnp.mean(o * o, axis=-1, keepdims=True) + EPS) * norm_w.astype(jnp.float32)
    zf = z.astype(jnp.float32).reshape(B, L, H, dv)
    return (on * jax.nn.silu(zf)).reshape(B, L, H * dv).astype(z.dtype)


def na_heads(p, q_norm, k_norm):
    B, L, _ = p.shape
    q, k, v = jnp.split(p, 3, axis=-1)

    def heads(t):
        return t.reshape(B, L, NA_HEADS, NA_HEAD_DIM).transpose(0, 2, 1, 3)

    return rmsnorm(heads(q), q_norm), rmsnorm(heads(k), k_norm), heads(v)


def na_col_tables():
    nb = GRID_W // NA_QB
    q_cols = np.arange(nb)[:, None] * NA_QB + np.arange(NA_QB)[None, :]
    k_start = np.clip(np.arange(nb) * NA_QB - NA_WIN_COLS // 2, 0, GRID_W - NA_KB_COLS)
    k_cols = k_start[:, None] + np.arange(NA_KB_COLS)[None, :]
    c0 = np.clip(q_cols - NA_WIN_COLS // 2, 0, GRID_W - NA_WIN_COLS)
    kc = k_cols[:, None, :]
    mask = (kc >= c0[..., None]) & (kc < c0[..., None] + NA_WIN_COLS)
    dc = np.clip(kc - q_cols[..., None] + NA_WIN_COLS - 1, 0, 2 * NA_WIN_COLS - 2)
    return k_cols, mask, dc


def neighbourhood_attention(q, k, v, kc, vc, rpb):
    B, H, L, dh = q.shape
    rows = L // GRID_W
    kr = min(NA_WIN_ROWS, rows)
    nb = GRID_W // NA_QB
    nk = kr * NA_KB_COLS
    k_cols, mask, dc = na_col_tables()
    key_mask = np.broadcast_to(mask[:, :, None, :], (nb, NA_QB, kr, NA_KB_COLS)).reshape(nb, NA_QB, nk)
    col_bias = rpb[:, :, dc]
    scale = dh ** -0.5

    def grid(t):
        return t.reshape(B, H, rows, GRID_W, dh)

    qg, kg, vg = grid(q), grid(k), grid(v)

    def band(t, r0):
        t = lax.dynamic_slice_in_dim(t, r0, kr, axis=2)[:, :, :, k_cols]
        return t.transpose(0, 1, 3, 2, 4, 5).reshape(B, H, nb, nk, dh)

    def row(r):
        r0 = jnp.clip(r - kr // 2, 0, rows - kr)
        kb, vb = band(kg, r0), band(vg, r0)
        qr = lax.dynamic_index_in_dim(qg, r, axis=2, keepdims=False).reshape(B, H, nb, NA_QB, dh)
        dr = r0 + jnp.arange(kr) - r + NA_WIN_ROWS - 1
        bias = jnp.take(col_bias, dr, axis=1).transpose(0, 2, 3, 1, 4).reshape(H, nb, NA_QB, nk)
        s_loc = jnp.einsum("bhnqd,bhnkd->bhnqk", qr, kb).astype(jnp.float32) * scale + bias
        s_loc = jnp.where(key_mask, s_loc, -jnp.inf)
        s_ctx = jnp.einsum("bhnqd,bhkd->bhnqk", qr, kc).astype(jnp.float32) * scale
        p = jax.nn.softmax(jnp.concatenate([s_loc, s_ctx], axis=-1), axis=-1).astype(v.dtype)
        o = (jnp.einsum("bhnqk,bhnkd->bhnqd", p[..., :nk], vb)
             + jnp.einsum("bhnqk,bhkd->bhnqd", p[..., nk:], vc))
        return o.reshape(B, H, GRID_W, dh)

    out = lax.map(row, jnp.arange(rows))
    return out.transpose(1, 2, 0, 3, 4).reshape(B, H, L, dh)


def context_attention(q, k, v):
    s = jnp.einsum("bhqd,bhkd->bhqk", q, k).astype(jnp.float32) * q.shape[-1] ** -0.5
    p = jax.nn.softmax(s, axis=-1).astype(v.dtype)
    return jnp.einsum("bhqk,bhkd->bhqd", p, v)


def token_mixers(p, pc, pool_w, pool_scale, conv_w, a_log, dt_bias, gdn_norm_w,
                 q_norm, k_norm, rpb, with_ctx_out):
    splits = [POOL_END, GQKV_END, GZ_END, GAB_END]
    pv, gqkv, gz, gab, na = jnp.split(p, splits, axis=-1)
    pvc, gqkvc, gzc, gabc, nac = jnp.split(pc, splits, axis=-1)
    y_a = multiscale_pool(pv, pool_w, pool_scale)
    lat_b = gdn_prepare(gqkv, gab, conv_w, a_log, dt_bias, rope=True)
    ctx_b = gdn_prepare(gqkvc, gabc, conv_w, a_log, dt_bias, rope=False)
    o_ctx_b, o_lat_b = gdn_bidirectional(ctx_b, lat_b)
    y_b = gdn_output(o_lat_b, gz, gdn_norm_w)
    q, k, v = na_heads(na, q_norm, k_norm)
    qc, kc, vc = na_heads(nac, q_norm, k_norm)
    y_c = merge_heads(neighbourhood_attention(q, k, v, kc, vc, rpb))
    y = jnp.concatenate([y_a, y_b, y_c], axis=-1)
    if not with_ctx_out:
        return y, None
    yc_a = multiscale_pool(pvc, pool_w, pool_scale)
    yc_b = gdn_output(o_ctx_b, gzc, gdn_norm_w)
    yc_c = merge_heads(context_attention(qc, kc, vc))
    return y, jnp.concatenate([yc_a, yc_b, yc_c], axis=-1)


def conv_ffn(h, w_up, conv_w, w_down):
    u = centred_dwconv(h @ w_up, conv_w)
    a, b = jnp.split(u, 2, axis=-1)
    return (jax.nn.silu(a) * b) @ w_down


def setup_inputs(seed: int = 0) -> dict:
    key = jax.random.key(seed)
    ks = jax.random.split(key, 22)
    f32 = jnp.float32

    def nrm(k, shape, std):
        return jax.random.normal(k, shape, f32) * std

    def gain(k, shape):
        return 1.0 + nrm(k, shape, 0.02)

    L, D = DEPTH, D_MODEL
    dt = jnp.exp(jax.random.uniform(ks[12], (L, 2, GDN_HEADS), f32, math.log(1e-3), math.log(1e-1)))
    return {
        "x": nrm(ks[0], (BATCH, SEQ, D), 1.0),
        "c": nrm(ks[1], (BATCH, D), 1.0),
        "ctx": nrm(ks[2], (BATCH, CTX_LEN, D), 1.0),
        "c_ctx": nrm(ks[3], (D,), 1.0),
        "w_ada": nrm(ks[4], (L, D, 6 * D), 0.5 * D ** -0.5),
        "b_ada": nrm(ks[5], (L, 6 * D), 0.01),
        "norm_mix": gain(ks[6], (L, D)),
        "w_in": nrm(ks[7], (L, D, N_IN), D ** -0.5),
        "pool_w": nrm(ks[8], (L, POOL_GROUPS, POOL_GROUP_DIM, POOL_GROUP_DIM), POOL_GROUP_DIM ** -0.5),
        "pool_scale": gain(ks[9], (L, POOL_WIDTH)),
        "gdn_conv": nrm(ks[10], (L, SHORT_CONV, 3 * GDN_WIDTH), SHORT_CONV ** -0.5),
        "gdn_a_log": jnp.log(jax.random.uniform(ks[11], (L, 2, GDN_HEADS), f32, 1.0, 16.0)),
        "gdn_dt_bias": dt + jnp.log(-jnp.expm1(-dt)),
        "gdn_norm": gain(ks[13], (L, GDN_HEAD_DIM)),
        "na_q_norm": gain(ks[14], (L, NA_HEAD_DIM)),
        "na_k_norm": gain(ks[15], (L, NA_HEAD_DIM)),
        "na_rpb": nrm(ks[16], (L, NA_HEADS, 2 * NA_WIN_ROWS - 1, 2 * NA_WIN_COLS - 1), 0.1),
        "w_out": nrm(ks[17], (L, MIX_WIDTH, D), MIX_WIDTH ** -0.5),
        "norm_ffn": gain(ks[18], (L, D)),
        "w_up": nrm(ks[19], (L, D, 2 * D_FF), D ** -0.5),
        "ffn_conv": nrm(ks[20], (L, FFN_CONV, 2 * D_FF), FFN_CONV ** -0.5),
        "w_down": nrm(ks[21], (L, D_FF, D), D_FF ** -0.5),
    }


def reference(x, c, ctx, c_ctx, w_ada, b_ada, norm_mix, w_in, pool_w, pool_scale, gdn_conv,
              gdn_a_log, gdn_dt_bias, gdn_norm, na_q_norm, na_k_norm, na_rpb, w_out,
              norm_ffn, w_up, ffn_conv, w_down):
    sc = jax.nn.silu(c)
    scc = jax.nn.silu(c_ctx)
    for l in range(DEPTH):
        with_ctx_out = l < DEPTH - 1
        mod = (sc @ w_ada[l] + b_ada[l])[:, None, :]
        mod_c = scc @ w_ada[l] + b_ada[l]
        sh1, s1, g1, sh2, s2, g2 = jnp.split(mod, 6, axis=-1)
        csh1, cs1, cg1, csh2, cs2, cg2 = jnp.split(mod_c, 6, axis=-1)
        h = rmsnorm(x, norm_mix[l]) * (1.0 + s1) + sh1
        hc = rmsnorm(ctx, norm_mix[l]) * (1.0 + cs1) + csh1
        y, yc = token_mixers(h @ w_in[l], hc @ w_in[l], pool_w[l], pool_scale[l], gdn_conv[l],
                             gdn_a_log[l], gdn_dt_bias[l], gdn_norm[l], na_q_norm[l],
                             na_k_norm[l], na_rpb[l], with_ctx_out)
        x = x + g1 * (y @ w_out[l])
        hf = rmsnorm(x, norm_ffn[l]) * (1.0 + s2) + sh2
        x = x + g2 * conv_ffn(hf, w_up[l], ffn_conv[l], w_down[l])
        if with_ctx_out:
            ctx = ctx + cg1 * (yc @ w_out[l])
            hfc = rmsnorm(ctx, norm_ffn[l]) * (1.0 + cs2) + csh2
            ctx = ctx + cg2 * conv_ffn(hfc, w_up[l], ffn_conv[l], w_down[l])
    return x
```

```python
import functools
import math

import numpy as np
import jax
import jax.numpy as jnp
from jax import lax
from jax.experimental import pallas as pl
from jax.experimental.pallas import tpu as pltpu

F32 = jnp.float32
BF16 = jnp.bfloat16

D_MODEL = 1024
SEQ = 4096
CTX = 256
T_ALL = CTX + SEQ
GRID_W = 64
EPS = 1e-6

POOL_W = 256
POOL_WINDOWS = (2, 4, 8, 16)
POOL_GD = 64
GDN_W = 512
GDN_H = 4
GDN_DH = 128
CHUNK = 64
SHORT_CONV = 5
ROPE_THETA = 10000.0
NA_W = 256
NA_H = 4
NA_DH = 64
NA_WIN_ROWS = 8
NA_WIN_COLS = 16
D_FF = 2816
FFN_CONV = 3

POOL_END = POOL_W
GQKV_END = POOL_END + 3 * GDN_W
GZ_END = GQKV_END + GDN_W
GAB_END = GZ_END + 4 * GDN_H
N_IN = GAB_END + 3 * NA_W

LANES = 128
SUBLANES = 8
TM = 256
N_TILES = T_ALL // TM
HALO = SUBLANES
N_CHUNKS = T_ALL // CHUNK
CPT = TM // CHUNK
FF_TILE = 256
NEG = -1e30
VMEM_LIMIT = 56 * 1024 * 1024

_HI = lax.Precision.HIGHEST


def _dot(a, b):
    return jnp.dot(a, b, preferred_element_type=F32)


def _dot_hi(a, b):
    return jnp.dot(a, b, preferred_element_type=F32, precision=_HI)


def _silu(x):
    return x * jax.nn.sigmoid(x)


def _params(sem, vmem=None):
    return pltpu.CompilerParams(dimension_semantics=sem, vmem_limit_bytes=vmem)


def _mod_spec(chunk, n_batch):
    def imap(b, i, *_):
        return (jnp.where(i == 0, n_batch, b), 0, chunk)
    return pl.BlockSpec((1, 1, D_MODEL), imap)


def _prev_halo_map(b, i, *_):
    return (b, jnp.maximum(i * (TM // HALO) - 1, 0), 0)


def _next_halo_map(b, i, *_):
    return (b, jnp.minimum((i + 1) * (TM // HALO), T_ALL // HALO - 1), 0)


def _halo_valid(i):
    prev_ok = jnp.logical_and(i != 0, i != 1)
    next_ok = jnp.logical_and(i != 0, i != N_TILES - 1)
    return prev_ok, next_ok


def _fill_halo_buf(buf, prev_ref, cur_ref, next_ref, i):
    prev_ok, next_ok = _halo_valid(i)
    buf[0:HALO, :] = jnp.where(prev_ok, prev_ref[0], 0.0)
    buf[HALO:HALO + TM, :] = cur_ref[0]
    buf[HALO + TM:2 * HALO + TM, :] = jnp.where(next_ok, next_ref[0], 0.0)


def _ada_kernel(c_ref, w_ref, b_ref, o_ref):
    sc = _silu(c_ref[...]).astype(BF16)
    o_ref[0] = _dot(sc, w_ref[0].astype(BF16)) + b_ref[0]


def _ada_call(cvec, w_ada, b_ada):
    depth = w_ada.shape[0]
    rows = cvec.shape[0]
    tn = 1536
    return pl.pallas_call(
        _ada_kernel,
        grid=(depth, 6 * D_MODEL // tn),
        in_specs=[
            pl.BlockSpec((rows, D_MODEL), lambda l, j: (0, 0)),
            pl.BlockSpec((1, D_MODEL, tn), lambda l, j: (l, 0, j)),
            pl.BlockSpec((1, 1, tn), lambda l, j: (l, 0, j)),
        ],
        out_specs=pl.BlockSpec((1, rows, tn), lambda l, j: (l, 0, j)),
        out_shape=jax.ShapeDtypeStruct((depth, rows, 6 * D_MODEL), F32),
        compiler_params=_params(("parallel", "parallel"), VMEM_LIMIT),
        name="adaln",
    )(cvec, w_ada, b_ada.reshape(depth, 1, 6 * D_MODEL))


def _inproj_kernel(x_ref, sh_ref, sc_ref, nw_ref, wp_ref, wq_ref, wz_ref, wab_ref, wna_ref,
                   qn_ref, kn_ref, pool_o, gqkv_o, gz_o, gab_o, naq_o, nak_o, nav_o):
    x = x_ref[0]
    r = lax.rsqrt(jnp.mean(x * x, axis=-1, keepdims=True) + EPS)
    h = (x * r * nw_ref[0]) * (1.0 + sc_ref[0]) + sh_ref[0]
    hb = h.astype(BF16)
    pool_o[0] = _dot(hb, wp_ref[0])
    gqkv_o[0] = _dot(hb, wq_ref[0])
    gz_o[0] = _dot(hb, wz_ref[0])
    gab_o[0] = _dot(hb, wab_ref[0])
    na = _dot(hb, wna_ref[0])
    li = lax.broadcasted_iota(jnp.int32, (NA_W, NA_W), 0) // NA_DH
    lj = lax.broadcasted_iota(jnp.int32, (NA_W, NA_W), 1) // NA_DH
    seg = (li == lj).astype(F32)
    q = na[:, 0:NA_W]
    k = na[:, NA_W:2 * NA_W]
    qs = _dot_hi(q * q, seg) * (1.0 / NA_DH)
    ks = _dot_hi(k * k, seg) * (1.0 / NA_DH)
    naq_o[0] = (q * lax.rsqrt(qs + EPS) * qn_ref[0] * (NA_DH ** -0.5)).astype(BF16)
    nak_o[0] = (k * lax.rsqrt(ks + EPS) * kn_ref[0]).astype(BF16)
    nav_o[0] = na[:, 2 * NA_W:3 * NA_W].astype(BF16)


def _inproj_call(l, xs, mod, norm_mix, wts, qn_t, kn_t):
    B = xs.shape[0]
    wp, wq, wz, wab, wna = wts

    def wspec(w):
        return pl.BlockSpec((1,) + w.shape[1:], lambda b, i: (l, 0, 0))

    def rowspec(n):
        return pl.BlockSpec((1, TM, n), lambda b, i: (b, i, 0))

    def vec(n):
        return pl.BlockSpec((1, 1, n), lambda b, i: (l, 0, 0))

    outs = [(POOL_W, F32), (3 * GDN_W, F32), (GDN_W, F32), (LANES, F32),
            (NA_W, BF16), (NA_W, BF16), (NA_W, BF16)]
    return pl.pallas_call(
        _inproj_kernel,
        grid=(B, N_TILES),
        in_specs=[rowspec(D_MODEL), _mod_spec(0, B), _mod_spec(1, B), vec(D_MODEL),
                  wspec(wp), wspec(wq), wspec(wz), wspec(wab), wspec(wna),
                  vec(NA_W), vec(NA_W)],
        out_specs=[rowspec(n) for n, _ in outs],
        out_shape=[jax.ShapeDtypeStruct((B, T_ALL, n), dt) for n, dt in outs],
        compiler_params=_params(("parallel", "parallel"), VMEM_LIMIT),
        name="inproj",
    )(xs, mod, mod, norm_mix, wp, wq, wz, wab, wna, qn_t, kn_t)


def _pool_kernel(cur_ref, prev_ref, next_ref, w_ref, s_ref, o_ref, buf):
    i = pl.program_id(1)
    _fill_halo_buf(buf, prev_ref, cur_ref, next_ref, i)
    v = cur_ref[0]
    row = lax.broadcasted_iota(jnp.int32, (TM, POOL_W), 0)
    lane = lax.broadcasted_iota(jnp.int32, (TM, POOL_W), 1)
    t = jnp.where(i == 0, row, row + (i - 1) * TM)
    seg_len = jnp.where(i == 0, CTX, SEQ)
    half = jnp.left_shift(1, lane // POOL_GD)
    cnt = (jnp.minimum(t + half, seg_len) - jnp.maximum(t - half, 0)).astype(F32)

    def sh(k):
        return buf[pl.ds(HALO + k, TM), :]

    acc = sh(-1) + v
    pooled = jnp.where(lane < POOL_GD, acc, 0.0)
    lo, hi = -1, 0
    for g in range(1, len(POOL_WINDOWS)):
        hw = POOL_WINDOWS[g] // 2
        for k in list(range(-hw, lo)) + list(range(hi + 1, hw)):
            acc = acc + sh(k)
        lo, hi = -hw, hw - 1
        pooled = jnp.where(lane // POOL_GD == g, acc, pooled)
    pooled = pooled / cnt - v
    o_ref[0] = _dot(pooled.astype(BF16), w_ref[0]) * s_ref[0]


def _pool_call(l, pv, w_bd, scale):
    B = pv.shape[0]
    return pl.pallas_call(
        _pool_kernel,
        grid=(B, N_TILES),
        in_specs=[
            pl.BlockSpec((1, TM, POOL_W), lambda b, i: (b, i, 0)),
            pl.BlockSpec((1, HALO, POOL_W), _prev_halo_map),
            pl.BlockSpec((1, HALO, POOL_W), _next_halo_map),
            pl.BlockSpec((1, POOL_W, POOL_W), lambda b, i: (l, 0, 0)),
            pl.BlockSpec((1, 1, POOL_W), lambda b, i: (l, 0, 0)),
        ],
        out_specs=pl.BlockSpec((1, TM, POOL_W), lambda b, i: (b, i, 0)),
        out_shape=jax.ShapeDtypeStruct((B, T_ALL, POOL_W), F32),
        scratch_shapes=[pltpu.VMEM((TM + 2 * HALO, POOL_W), F32)],
        compiler_params=_params(("parallel", "parallel")),
        name="pool",
    )(pv, pv, pv, w_bd, scale)


def _tri_masks(n, upper):
    i = lax.broadcasted_iota(jnp.int32, (n, n), 0)
    j = lax.broadcasted_iota(jnp.int32, (n, n), 1)
    if upper:
        return j >= i, j > i
    return j <= i, j < i


def _bdot(a, b, hi=False):
    return lax.dot_general(a, b, (((2,), (1,)), ((0,), (0,))), preferred_element_type=F32,
                           precision=_HI if hi else None)


def _bdot_nt(a, b):
    return lax.dot_general(a, b, (((2,), (2,)), ((0,), (0,))), preferred_element_type=F32)


def _gdn_prep_kernel(qc_ref, qp_ref, qn_ref, kc_ref, kp_ref, kn_ref, vc_ref, vp_ref, vn_ref,
                     gab_ref, cw_ref, alog_ref, dtb_ref, cos_ref, sin_ref,
                     uf_o, ub_o, wf_o, wb_o, qdf_o, qdb_o, kdf_o, kdb_o, inf_o, inb_o, gl_o,
                     buf, gt_buf):
    i = pl.program_id(1)
    h = pl.program_id(2)

    def conv_act(cur, prev, nxt, part):
        _fill_halo_buf(buf, prev, cur, nxt, i)
        acc = jnp.zeros((TM, GDN_DH), F32)
        for k in range(SHORT_CONV):
            acc = acc + buf[pl.ds(HALO + k - SHORT_CONV // 2, TM), :] * cw_ref[0, part, k:k + 1, :]
        return _silu(acc)

    q = conv_act(qc_ref, qp_ref, qn_ref, 0)
    k = conv_act(kc_ref, kp_ref, kn_ref, 1)
    v = conv_act(vc_ref, vp_ref, vn_ref, 2)

    lane = lax.broadcasted_iota(jnp.int32, (TM, GDN_DH), 1)
    first = (lane % (GDN_DH // 2)) < (GDN_DH // 4)
    cos = cos_ref[...]
    sin = sin_ref[...]

    def norm_rope(xv):
        xn = xv * lax.rsqrt(jnp.sum(xv * xv, axis=-1, keepdims=True) + EPS)
        partner = jnp.where(first, pltpu.roll(xn, GDN_DH - GDN_DH // 4, axis=1),
                            pltpu.roll(xn, GDN_DH // 4, axis=1))
        return xn * cos + partner * sin

    q = norm_rope(q) * (GDN_DH ** -0.5)
    k = norm_rope(k)

    gab = gab_ref[0]
    g_all = -jnp.exp(alog_ref[0]) * jax.nn.softplus(gab + dtb_ref[0])
    beta_all = jax.nn.sigmoid(gab)
    ri = lax.broadcasted_iota(jnp.int32, (TM, TM), 0)
    rj = lax.broadcasted_iota(jnp.int32, (TM, TM), 1)
    same = (ri // CHUNK) == (rj // CHUNK)
    low = jnp.logical_and(same, rj <= ri).astype(F32)
    upp = jnp.logical_and(same, rj >= ri).astype(F32)
    pre = _dot_hi(low, g_all)
    suf = _dot_hi(upp, g_all)
    glane = lax.broadcasted_iota(jnp.int32, (TM, LANES), 1)
    gc_all = jnp.where(glane < GDN_H, pre, suf)
    gt_buf[...] = gc_all.T

    def pick(arr, idx):
        return jnp.sum(jnp.where(glane == idx, arr, 0.0), axis=1, keepdims=True)

    k3 = k.reshape(CPT, CHUNK, GDN_DH)
    q3 = q.reshape(CPT, CHUNK, GDN_DH)
    v3 = v.reshape(CPT, CHUNK, GDN_DH)
    k3b = k3.astype(BF16)
    eye = (lax.broadcasted_iota(jnp.int32, (CHUNK, CHUNK), 0)
           == lax.broadcasted_iota(jnp.int32, (CHUNK, CHUNK), 1)).astype(F32)

    outs = ((uf_o, wf_o, qdf_o, kdf_o, inf_o), (ub_o, wb_o, qdb_o, kdb_o, inb_o))
    for d in range(2):
        u_o, w_o, qd_o, kd_o, in_o = outs[d]
        incl, strict = _tri_masks(CHUNK, upper=(d == 1))
        gcol = pick(gc_all, d * GDN_H + h).reshape(CPT, CHUNK, 1)
        beta = pick(beta_all, 2 * GDN_H + d * GDN_H + h).reshape(CPT, CHUNK, 1)
        grow_full = gt_buf[pl.ds(d * GDN_H + h, 1), :]
        grow = jnp.stack([grow_full[:, c * CHUNK:(c + 1) * CHUNK] for c in range(CPT)], axis=0)
        diff = gcol - grow
        decay = jnp.where(incl, jnp.exp(jnp.where(incl, diff, 0.0)), 0.0)
        kb = k3 * beta
        lhs = jnp.concatenate([kb, q3], axis=1).astype(BF16)
        m1 = _bdot_nt(lhs, k3b)
        a_mat = jnp.where(strict, m1[:, :CHUNK] * decay, 0.0)
        intra = m1[:, CHUNK:] * decay
        p = -a_mat
        t_inv = eye + p
        for _ in range(5):
            p = _bdot(p, p, hi=True)
            t_inv = t_inv + _bdot(t_inv, p, hi=True)
        e = jnp.exp(gcol)
        rhs = jnp.concatenate([v3 * beta, k3 * (beta * e)], axis=2)
        uw = _bdot(t_inv, rhs, hi=True)
        g_last = gcol[:, CHUNK - 1:CHUNK, :] if d == 0 else gcol[:, 0:1, :]
        u_o[0] = uw[:, :, :GDN_DH].reshape(TM, GDN_DH)
        w_o[0] = uw[:, :, GDN_DH:].reshape(TM, GDN_DH).astype(BF16)
        qd_o[0] = (q3 * e).reshape(TM, GDN_DH).astype(BF16)
        kd_o[0] = (k3 * jnp.exp(g_last - gcol)).reshape(TM, GDN_DH).astype(BF16)
        in_o[0, 0] = intra.reshape(TM, CHUNK).astype(BF16)
        gl_o[0, 0, :, d:d + 1, :] = jnp.broadcast_to(jnp.exp(g_last), (CPT, 1, LANES))


def _gdn_prep_call(l, gqkv, gab, conv_w, alog, dtb, cos_t, sin_t):
    B = gqkv.shape[0]

    def cur(part):
        return pl.BlockSpec((1, TM, GDN_DH), lambda b, i, h: (b, i, part * GDN_H + h))

    def prev(part):
        return pl.BlockSpec((1, HALO, GDN_DH),
                            lambda b, i, h: _prev_halo_map(b, i)[:2] + (part * GDN_H + h,))

    def nxt(part):
        return pl.BlockSpec((1, HALO, GDN_DH),
                            lambda b, i, h: _next_halo_map(b, i)[:2] + (part * GDN_H + h,))

    def vec():
        return pl.BlockSpec((1, 1, LANES), lambda b, i, h: (l, 0, 0))

    row_h = pl.BlockSpec((1, TM, GDN_DH), lambda b, i, h: (b, i, h))
    in_spec = pl.BlockSpec((1, 1, TM, CHUNK), lambda b, i, h: (b, h, i, 0))
    f32o = jax.ShapeDtypeStruct((B, T_ALL, GDN_W), F32)
    bf16o = jax.ShapeDtypeStruct((B, T_ALL, GDN_W), BF16)
    ino = jax.ShapeDtypeStruct((B, GDN_H, T_ALL, CHUNK), BF16)
    glo = jax.ShapeDtypeStruct((B, GDN_H, N_CHUNKS, 2, LANES), F32)
    return pl.pallas_call(
        _gdn_prep_kernel,
        grid=(B, N_TILES, GDN_H),
        in_specs=[cur(0), prev(0), nxt(0), cur(1), prev(1), nxt(1), cur(2), prev(2), nxt(2),
                  pl.BlockSpec((1, TM, LANES), lambda b, i, h: (b, i, 0)),
                  pl.BlockSpec((1, 3, SHORT_CONV, GDN_DH), lambda b, i, h: (l * GDN_H + h, 0, 0, 0)),
                  vec(), vec(),
                  pl.BlockSpec((TM, GDN_DH), lambda b, i, h: (i, 0)),
                  pl.BlockSpec((TM, GDN_DH), lambda b, i, h: (i, 0))],
        out_specs=[row_h, row_h, row_h, row_h, row_h, row_h, row_h, row_h, in_spec, in_spec,
                   pl.BlockSpec((1, 1, CPT, 2, LANES), lambda b, i, h: (b, h, i, 0, 0))],
        out_shape=[f32o, f32o, bf16o, bf16o, bf16o, bf16o, bf16o, bf16o, ino, ino, glo],
        scratch_shapes=[pltpu.VMEM((TM + 2 * HALO, GDN_DH), F32), pltpu.VMEM((LANES, TM), F32)],
        compiler_params=_params(("parallel", "parallel", "arbitrary"), VMEM_LIMIT),
        name="gdn_prep",
    )(gqkv, gqkv, gqkv, gqkv, gqkv, gqkv, gqkv, gqkv, gqkv, gab, conv_w, alog, dtb, cos_t, sin_t)


def _gdn_scan_kernel(uf_ref, ub_ref, wf_ref, wb_ref, qdf_ref, qdb_ref, kdf_ref, kdb_ref,
                     inf_ref, inb_ref, glf_ref, glb_ref, of_o, ob_o, s_ref):
    i = pl.program_id(1)

    @pl.when(i == 0)
    def _():
        s_ref[...] = jnp.zeros_like(s_ref)

    refs = ((uf_ref, wf_ref, qdf_ref, kdf_ref, inf_ref, glf_ref, of_o),
            (ub_ref, wb_ref, qdb_ref, kdb_ref, inb_ref, glb_ref, ob_o))
    for step in range(CPT):
        for d in range(2):
            u_ref, w_ref, qd_ref, kd_ref, in_ref, gl_ref, o_ref = refs[d]
            c = step if d == 0 else CPT - 1 - step
            rows = slice(c * CHUNK, (c + 1) * CHUNK)
            for h in range(GDN_H):
                cols = slice(h * GDN_DH, (h + 1) * GDN_DH)
                s = s_ref[d * GDN_H + h]
                wq = jnp.concatenate([w_ref[0, rows, cols], qd_ref[0, rows, cols]], axis=0)
                ws_qs = _dot(wq, s.astype(BF16))
                v_new = u_ref[0, rows, cols] - ws_qs[:CHUNK]
                vb = v_new.astype(BF16)
                o_ref[0, rows, cols] = ws_qs[CHUNK:] + _dot(in_ref[0, h, rows, :], vb)
                upd = lax.dot_general(kd_ref[0, rows, cols], vb, (((0,), (0,)), ((), ())),
                                      preferred_element_type=F32)
                s_ref[d * GDN_H + h] = s * gl_ref[0, h, c, d:d + 1, :] + upd


def _gdn_scan_call(prep):
    uf, ub, wf, wb, qdf, qdb, kdf, kdb, inf, inb, gl = prep
    B = uf.shape[0]

    def rev(i):
        return jnp.where(i == 0, 0, N_TILES - i)

    fwd = pl.BlockSpec((1, TM, GDN_W), lambda b, i: (b, i, 0))
    bwd = pl.BlockSpec((1, TM, GDN_W), lambda b, i: (b, rev(i), 0))
    in_f = pl.BlockSpec((1, GDN_H, TM, CHUNK), lambda b, i: (b, 0, i, 0))
    in_b = pl.BlockSpec((1, GDN_H, TM, CHUNK), lambda b, i: (b, 0, rev(i), 0))
    gl_f = pl.BlockSpec((1, GDN_H, CPT, 2, LANES), lambda b, i: (b, 0, i, 0, 0))
    gl_b = pl.BlockSpec((1, GDN_H, CPT, 2, LANES), lambda b, i: (b, 0, rev(i), 0, 0))
    o_shape = jax.ShapeDtypeStruct((B, T_ALL, GDN_W), F32)
    return pl.pallas_call(
        _gdn_scan_kernel,
        grid=(B, N_TILES),
        in_specs=[fwd, bwd, fwd, bwd, fwd, bwd, fwd, bwd, in_f, in_b, gl_f, gl_b],
        out_specs=[fwd, bwd],
        out_shape=[o_shape, o_shape],
        scratch_shapes=[pltpu.VMEM((2 * GDN_H, GDN_DH, GDN_DH), F32)],
        compiler_params=_params(("parallel", "arbitrary"), VMEM_LIMIT),
        name="gdn_scan",
    )(uf, ub, wf, wb, qdf, qdb, kdf, kdb, inf, inb, gl, gl)


NA_ROWS = SEQ // GRID_W
NA_CTX_BLOCKS = CTX // GRID_W
NA_BAND = NA_WIN_ROWS * GRID_W


def _na_kernel(q_ref, k_ref, v_ref, bias_ref, o_ref):
    step = pl.program_id(1)
    q = q_ref[0]
    kc = k_ref[0, 0:CTX, :]
    vc = v_ref[0, 0:CTX, :]
    nt = (((1,), (1,)), ((), ()))

    @pl.when(step < NA_CTX_BLOCKS)
    def _():
        outs = []
        for h in range(NA_H):
            cols = slice(h * NA_DH, (h + 1) * NA_DH)
            s = lax.dot_general(q[:, cols], kc[:, cols], nt, preferred_element_type=F32)
            m = jnp.max(s, axis=-1, keepdims=True)
            p = jnp.exp(s - m)
            den = jnp.sum(p, axis=-1, keepdims=True)
            outs.append(_dot(p.astype(BF16), vc[:, cols]) / den)
        o_ref[0] = jnp.concatenate(outs, axis=-1)

    @pl.when(step >= NA_CTX_BLOCKS)
    def _():
        r = step - NA_CTX_BLOCKS
        r0 = jnp.clip(r - NA_WIN_ROWS // 2, 0, NA_ROWS - NA_WIN_ROWS)
        d0 = r0 - r + NA_WIN_ROWS - 1
        start = pl.multiple_of(CTX + r0 * GRID_W, GRID_W)
        kb = k_ref[0, pl.ds(start, NA_BAND), :]
        vb = v_ref[0, pl.ds(start, NA_BAND), :]
        outs = []
        for h in range(NA_H):
            cols = slice(h * NA_DH, (h + 1) * NA_DH)
            qh = q[:, cols]
            s_loc = lax.dot_general(qh, kb[:, cols], nt, preferred_element_type=F32) + bias_ref[h, d0]
            s_ctx = lax.dot_general(qh, kc[:, cols], nt, preferred_element_type=F32)
            m = jnp.maximum(jnp.max(s_loc, axis=-1, keepdims=True),
                            jnp.max(s_ctx, axis=-1, keepdims=True))
            p_loc = jnp.exp(s_loc - m)
            p_ctx = jnp.exp(s_ctx - m)
            den = jnp.sum(p_loc, axis=-1, keepdims=True) + jnp.sum(p_ctx, axis=-1, keepdims=True)
            o = _dot(p_loc.astype(BF16), vb[:, cols]) + _dot(p_ctx.astype(BF16), vc[:, cols])
            outs.append(o / den)
        o_ref[0] = jnp.concatenate(outs, axis=-1)


def _na_call(q, k, v, bias):
    B = q.shape[0]
    full = pl.BlockSpec((1, T_ALL, NA_W), lambda b, s: (b, 0, 0))
    blk = pl.BlockSpec((1, GRID_W, NA_W), lambda b, s: (b, s, 0))
    return pl.pallas_call(
        _na_kernel,
        grid=(B, T_ALL // GRID_W),
        in_specs=[blk, full, full,
                  pl.BlockSpec(bias.shape, lambda b, s: (0, 0, 0, 0))],
        out_specs=blk,
        out_shape=jax.ShapeDtypeStruct((B, T_ALL, NA_W), F32),
        compiler_params=_params(("parallel", "arbitrary"), VMEM_LIMIT),
        name="natten",
    )(q, k, v, bias)


def _na_bias_table(rpb):
    qc = np.arange(GRID_W)[:, None]
    kc = np.arange(GRID_W)[None, :]
    c0 = np.clip(qc - NA_WIN_COLS // 2, 0, GRID_W - NA_WIN_COLS)
    mask = (kc >= c0) & (kc < c0 + NA_WIN_COLS)
    dc = np.clip(kc - qc + NA_WIN_COLS - 1, 0, 2 * NA_WIN_COLS - 2)
    dr = np.arange(NA_WIN_ROWS)[:, None] + np.arange(NA_WIN_ROWS)[None, :]
    t = rpb[:, dr][:, :, :, dc]
    t = jnp.where(mask[None, None, None], t, NEG)
    return t.transpose(0, 1, 3, 2, 4).reshape(NA_H, NA_WIN_ROWS, GRID_W, NA_BAND)


def _outproj_kernel(x_ref, ya_ref, of_ref, ob_ref, z_ref, yc_ref, g_ref, gn_ref, w_ref, o_ref):
    acc = _dot(ya_ref[0].astype(BF16), w_ref[0, 0:POOL_W, :])
    o = of_ref[0] + ob_ref[0]
    z = z_ref[0]
    for h in range(GDN_H):
        cols = slice(h * GDN_DH, (h + 1) * GDN_DH)
        oh = o[:, cols]
        on = oh * lax.rsqrt(jnp.mean(oh * oh, axis=-1, keepdims=True) + EPS) * gn_ref[0]
        yb = (on * _silu(z[:, cols])).astype(BF16)
        acc = acc + _dot(yb, w_ref[0, POOL_W + h * GDN_DH:POOL_W + (h + 1) * GDN_DH, :])
    acc = acc + _dot(yc_ref[0].astype(BF16), w_ref[0, POOL_W + GDN_W:, :])
    o_ref[0] = x_ref[0] + g_ref[0] * acc


def _outproj_call(l, xs, mod, ya, o_f, o_b, gz, yc, gdn_norm, w_out):
    B = xs.shape[0]

    def rowspec(n):
        return pl.BlockSpec((1, TM, n), lambda b, i: (b, i, 0))

    return pl.pallas_call(
        _outproj_kernel,
        grid=(B, N_TILES),
        in_specs=[rowspec(D_MODEL), rowspec(POOL_W), rowspec(GDN_W), rowspec(GDN_W), rowspec(GDN_W),
                  rowspec(NA_W), _mod_spec(2, B),
                  pl.BlockSpec((1, 1, GDN_DH), lambda b, i: (l, 0, 0)),
                  pl.BlockSpec((1, D_MODEL, D_MODEL), lambda b, i: (l, 0, 0))],
        out_specs=rowspec(D_MODEL),
        out_shape=jax.ShapeDtypeStruct(xs.shape, F32),
        input_output_aliases={0: 0},
        compiler_params=_params(("parallel", "parallel"), VMEM_LIMIT),
        name="outproj",
    )(xs, ya, o_f, o_b, gz, yc, mod, gdn_norm, w_out)


def _ffn_kernel(cur_ref, prev_ref, next_ref, sh_ref, sc_ref, g_ref, nw_ref, wu_ref, cw_ref, wd_ref,
                o_ref, xbuf, hbuf, ua_buf, ub_buf, acc):
    i = pl.program_id(1)
    _fill_halo_buf(xbuf, prev_ref, cur_ref, next_ref, i)
    x = xbuf[...]
    r = lax.rsqrt(jnp.mean(x * x, axis=-1, keepdims=True) + EPS)
    h = (x * r * nw_ref[0]) * (1.0 + sc_ref[0]) + sh_ref[0]
    prev_ok, next_ok = _halo_valid(i)
    rows = lax.broadcasted_iota(jnp.int32, (TM + 2 * HALO, 1), 0)
    ok = jnp.logical_and(jnp.logical_or(rows >= HALO, prev_ok),
                         jnp.logical_or(rows < HALO + TM, next_ok))
    hbuf[...] = jnp.where(ok, h, 0.0).astype(BF16)
    acc[...] = jnp.zeros_like(acc)
    for j in range(D_FF // FF_TILE):
        ca = slice(j * FF_TILE, (j + 1) * FF_TILE)
        cb = slice(D_FF + j * FF_TILE, D_FF + (j + 1) * FF_TILE)
        hb = hbuf[...]
        ua_buf[...] = _dot(hb, wu_ref[0, :, ca])
        ub_buf[...] = _dot(hb, wu_ref[0, :, cb])
        a = jnp.zeros((TM, FF_TILE), F32)
        b = jnp.zeros((TM, FF_TILE), F32)
        for k in range(FFN_CONV):
            off = HALO + k - FFN_CONV // 2
            a = a + ua_buf[pl.ds(off, TM), :] * cw_ref[0, k:k + 1, ca]
            b = b + ub_buf[pl.ds(off, TM), :] * cw_ref[0, k:k + 1, cb]
        act = (_silu(a) * b).astype(BF16)
        acc[...] += _dot(act, wd_ref[0, ca, :])
    o_ref[0] = cur_ref[0] + g_ref[0] * acc[...]


def _ffn_call(l, xs, mod, norm_ffn, w_up, ffn_conv, w_down):
    B = xs.shape[0]
    const = dict(pipeline_mode=pl.Buffered(1))
    return pl.pallas_call(
        _ffn_kernel,
        grid=(B, N_TILES),
        in_specs=[pl.BlockSpec((1, TM, D_MODEL), lambda b, i: (b, i, 0)),
                  pl.BlockSpec((1, HALO, D_MODEL), _prev_halo_map),
                  pl.BlockSpec((1, HALO, D_MODEL), _next_halo_map),
                  _mod_spec(3, B), _mod_spec(4, B), _mod_spec(5, B),
                  pl.BlockSpec((1, 1, D_MODEL), lambda b, i: (l, 0, 0)),
                  pl.BlockSpec((1, D_MODEL, 2 * D_FF), lambda b, i: (l, 0, 0), **const),
                  pl.BlockSpec((1, FFN_CONV, 2 * D_FF), lambda b, i: (l, 0, 0)),
                  pl.BlockSpec((1, D_FF, D_MODEL), lambda b, i: (l, 0, 0), **const)],
        out_specs=pl.BlockSpec((1, TM, D_MODEL), lambda b, i: (b, i, 0)),
        out_shape=jax.ShapeDtypeStruct(xs.shape, F32),
        scratch_shapes=[pltpu.VMEM((TM + 2 * HALO, D_MODEL), F32),
                        pltpu.VMEM((TM + 2 * HALO, D_MODEL), BF16),
                        pltpu.VMEM((TM + 2 * HALO, FF_TILE), F32),
                        pltpu.VMEM((TM + 2 * HALO, FF_TILE), F32),
                        pltpu.VMEM((TM, D_MODEL), F32)],
        compiler_params=_params(("parallel", "parallel"), VMEM_LIMIT),
        name="convffn",
    )(xs, xs, xs, mod, mod, mod, norm_ffn, w_up, ffn_conv, w_down)


def _rope_tables():
    half = GDN_DH // 2
    nf = half // 2
    t = jnp.arange(SEQ)
    freqs = ROPE_THETA ** (-jnp.arange(nf, dtype=F32) / nf)

    def part(pos):
        ang = pos.astype(F32)[:, None] * freqs
        cos, sin = jnp.cos(ang), jnp.sin(ang)
        return jnp.concatenate([cos, cos], axis=-1), jnp.concatenate([-sin, sin], axis=-1)

    c_r, s_r = part(t // GRID_W)
    c_c, s_c = part(t % GRID_W)
    cos = jnp.concatenate([c_r, c_c], axis=-1)
    sin = jnp.concatenate([s_r, s_c], axis=-1)
    cos = jnp.concatenate([jnp.ones((CTX, GDN_DH), F32), cos], axis=0)
    sin = jnp.concatenate([jnp.zeros((CTX, GDN_DH), F32), sin], axis=0)
    return cos, sin


def _pad_lanes(a, n=LANES):
    return jnp.pad(a, [(0, 0)] * (a.ndim - 1) + [(0, n - a.shape[-1])])


def kernel(x, c, ctx, c_ctx, w_ada, b_ada, norm_mix, w_in, pool_w, pool_scale, gdn_conv, gdn_a_log,
           gdn_dt_bias, gdn_norm, na_q_norm, na_k_norm, na_rpb, w_out, norm_ffn, w_up, ffn_conv, w_down):
    B = x.shape[0]
    depth = w_ada.shape[0]

    rows = -(-(B + 1) // SUBLANES) * SUBLANES
    cvec = jnp.concatenate([c, c_ctx[None, :], jnp.zeros((rows - B - 1, D_MODEL), F32)], axis=0)
    w_in_b = w_in.astype(BF16)
    wts = (w_in_b[:, :, :POOL_END], w_in_b[:, :, POOL_END:GQKV_END], w_in_b[:, :, GQKV_END:GZ_END],
           _pad_lanes(w_in_b[:, :, GZ_END:GAB_END]), w_in_b[:, :, GAB_END:])
    eye_g = jnp.eye(len(POOL_WINDOWS), dtype=F32)
    pool_bd = jnp.einsum("lgcd,gh->lgchd", pool_w, eye_g).reshape(depth, POOL_W, POOL_W).astype(BF16)
    conv_w = gdn_conv.reshape(depth, SHORT_CONV, 3, GDN_H, GDN_DH).transpose(0, 3, 2, 1, 4)
    conv_w = conv_w.reshape(depth * GDN_H, 3, SHORT_CONV, GDN_DH)
    alog = _pad_lanes(gdn_a_log.reshape(depth, 1, 2 * GDN_H))
    dtb = _pad_lanes(gdn_dt_bias.reshape(depth, 1, 2 * GDN_H))
    cos_t, sin_t = _rope_tables()
    qn_t = jnp.tile(na_q_norm, (1, NA_H)).reshape(depth, 1, NA_W)
    kn_t = jnp.tile(na_k_norm, (1, NA_H)).reshape(depth, 1, NA_W)
    w_out_b = w_out.astype(BF16)
    w_up_b = w_up.astype(BF16)
    w_down_b = w_down.astype(BF16)
    norm_mix3 = norm_mix.reshape(depth, 1, D_MODEL)
    norm_ffn3 = norm_ffn.reshape(depth, 1, D_MODEL)
    pool_scale3 = pool_scale.reshape(depth, 1, POOL_W)
    gdn_norm3 = gdn_norm.reshape(depth, 1, GDN_DH)

    mods = _ada_call(cvec, w_ada, b_ada).reshape(depth, rows, 1, 6 * D_MODEL)
    xs = jnp.concatenate([ctx, x], axis=1)

    for l in range(depth):
        mod = mods[l]
        pv, gqkv, gz, gab, naq, nak, nav = _inproj_call(l, xs, mod, norm_mix3, wts, qn_t, kn_t)
        ya = _pool_call(l, pv, pool_bd, pool_scale3)
        prep = _gdn_prep_call(l, gqkv, gab, conv_w, alog, dtb, cos_t, sin_t)
        o_f, o_b = _gdn_scan_call(prep)
        yc = _na_call(naq, nak, nav, _na_bias_table(na_rpb[l]))
        xs = _outproj_call(l, xs, mod, ya, o_f, o_b, gz, yc, gdn_norm3, w_out_b)
        xs = _ffn_call(l, xs, mod, norm_ffn3, w_up_b, ffn_conv, w_down_b)
    return xs[:, CTX:, :]
```

```python
import functools

import numpy as np
import jax
import jax.numpy as jnp
from jax import lax
from jax.experimental import pallas as pl
from jax.experimental.pallas import tpu as pltpu

F32 = jnp.float32
BF16 = jnp.bfloat16

D_MODEL = 1024
SEQ = 4096
CTX = 256
GRID_W = 64
EPS = 1e-6

POOL_W = 256
POOL_WINDOWS = (2, 4, 8, 16)
POOL_GD = 64
GDN_W = 512
GDN_H = 4
GDN_DH = 128
CHUNK = 64
SHORT_CONV = 5
ROPE_THETA = 10000.0
NA_W = 256
NA_H = 4
NA_DH = 64
NA_WIN_ROWS = 8
NA_WIN_COLS = 16
D_FF = 2816
FFN_CONV = 3

POOL_END = POOL_W
GQKV_END = POOL_END + 3 * GDN_W
GZ_END = GQKV_END + GDN_W
GAB_END = GZ_END + 4 * GDN_H

LANES = 128
SUBLANES = 8
HALO = SUBLANES
TM_MM = 512
TM_GDN = 256
FF_TILE = 256
NEG = -1e30
VMEM_LIMIT = 56 * 1024 * 1024

_HI = lax.Precision.HIGHEST
_SINGLE = dict(pipeline_mode=pl.Buffered(1))


def _dot(a, b):
    return jnp.dot(a, b, preferred_element_type=F32)


def _dot_hi(a, b):
    return jnp.dot(a, b, preferred_element_type=F32, precision=_HI)


def _silu(x):
    return x * jax.nn.sigmoid(x)


def _params(sem, vmem=VMEM_LIMIT):
    return pltpu.CompilerParams(dimension_semantics=sem, vmem_limit_bytes=vmem)


def _tile(rows):
    return min(rows, TM_MM)


def _mod_spec(chunk, row):
    return pl.BlockSpec((1, 1, D_MODEL), lambda b, i: (b if row is None else row, 0, chunk))


def _halo_specs(tm, rows, width, col=None):
    nb = tm // HALO
    last = rows // HALO - 1
    c = 0 if col is None else col
    prev = pl.BlockSpec((1, HALO, width), lambda b, i: (b, jnp.maximum(i * nb - 1, 0), c))
    nxt = pl.BlockSpec((1, HALO, width), lambda b, i: (b, jnp.minimum((i + 1) * nb, last), c))
    return prev, nxt


def _fill_halo_buf(buf, prev_ref, cur_ref, next_ref, i, n_tiles, tm):
    buf[0:HALO, :] = jnp.where(i > 0, prev_ref[0], 0.0)
    buf[HALO:HALO + tm, :] = cur_ref[0]
    buf[HALO + tm:2 * HALO + tm, :] = jnp.where(i < n_tiles - 1, next_ref[0], 0.0)


def _ada_kernel(c_ref, w_ref, b_ref, o_ref):
    sc = _silu(c_ref[...]).astype(BF16)
    o_ref[0] = _dot(sc, w_ref[0].astype(BF16)) + b_ref[0]


def _ada_call(cvec, w_ada, b_ada):
    depth = w_ada.shape[0]
    rows = cvec.shape[0]
    tn = 1536
    return pl.pallas_call(
        _ada_kernel,
        grid=(depth, 6 * D_MODEL // tn),
        in_specs=[
            pl.BlockSpec((rows, D_MODEL), lambda l, j: (0, 0)),
            pl.BlockSpec((1, D_MODEL, tn), lambda l, j: (l, 0, j)),
            pl.BlockSpec((1, 1, tn), lambda l, j: (l, 0, j)),
        ],
        out_specs=pl.BlockSpec((1, rows, tn), lambda l, j: (l, 0, j)),
        out_shape=jax.ShapeDtypeStruct((depth, rows, 6 * D_MODEL), F32),
        compiler_params=_params(("parallel", "parallel")),
        name="adaln",
    )(cvec, w_ada, b_ada.reshape(depth, 1, 6 * D_MODEL))


def _inproj_kernel(x_ref, sh_ref, sc_ref, nw_ref, wp_ref, wq_ref, wz_ref, wab_ref, wna_ref,
                   qn_ref, kn_ref, pool_o, gqkv_o, gz_o, gab_o, naq_o, nak_o, nav_o):
    x = x_ref[0]
    r = lax.rsqrt(jnp.mean(x * x, axis=-1, keepdims=True) + EPS)
    h = (x * r * nw_ref[0]) * (1.0 + sc_ref[0]) + sh_ref[0]
    hb = h.astype(BF16)
    pool_o[0] = _dot(hb, wp_ref[0])
    gqkv_o[0] = _dot(hb, wq_ref[0])
    gz_o[0] = _dot(hb, wz_ref[0])
    gab_o[0] = _dot(hb, wab_ref[0])
    na = _dot(hb, wna_ref[0])
    li = lax.broadcasted_iota(jnp.int32, (NA_W, NA_W), 0) // NA_DH
    lj = lax.broadcasted_iota(jnp.int32, (NA_W, NA_W), 1) // NA_DH
    seg = (li == lj).astype(F32)
    q = na[:, 0:NA_W]
    k = na[:, NA_W:2 * NA_W]
    qs = _dot_hi(q * q, seg) * (1.0 / NA_DH)
    ks = _dot_hi(k * k, seg) * (1.0 / NA_DH)
    naq_o[0] = (q * lax.rsqrt(qs + EPS) * qn_ref[0] * (NA_DH ** -0.5)).astype(BF16)
    nak_o[0] = (k * lax.rsqrt(ks + EPS) * kn_ref[0]).astype(BF16)
    nav_o[0] = na[:, 2 * NA_W:3 * NA_W].astype(BF16)


def _inproj_call(l, xs, mod, mod_row, norm_mix, wts, qn_t, kn_t):
    B, rows, _ = xs.shape
    tm = _tile(rows)

    def wspec(w):
        return pl.BlockSpec((1,) + w.shape[1:], lambda b, i: (l, 0, 0), **_SINGLE)

    def rowspec(n):
        return pl.BlockSpec((1, tm, n), lambda b, i: (b, i, 0))

    def vec(n):
        return pl.BlockSpec((1, 1, n), lambda b, i: (l, 0, 0))

    outs = [(POOL_W, F32), (3 * GDN_W, F32), (GDN_W, F32), (LANES, F32),
            (NA_W, BF16), (NA_W, BF16), (NA_W, BF16)]
    return pl.pallas_call(
        _inproj_kernel,
        grid=(B, rows // tm),
        in_specs=[rowspec(D_MODEL), _mod_spec(0, mod_row), _mod_spec(1, mod_row), vec(D_MODEL)]
        + [wspec(w) for w in wts] + [vec(NA_W), vec(NA_W)],
        out_specs=[rowspec(n) for n, _ in outs],
        out_shape=[jax.ShapeDtypeStruct((B, rows, n), dt) for n, dt in outs],
        compiler_params=_params(("parallel", "parallel")),
        name="inproj",
    )(xs, mod, mod, norm_mix, *wts, qn_t, kn_t)


def _pool_kernel(cur_ref, prev_ref, next_ref, w_ref, s_ref, o_ref, buf, *, tm, rows):
    i = pl.program_id(1)
    _fill_halo_buf(buf, prev_ref, cur_ref, next_ref, i, rows // tm, tm)
    v = cur_ref[0]
    t = lax.broadcasted_iota(jnp.int32, (tm, POOL_W), 0) + i * tm
    lane = lax.broadcasted_iota(jnp.int32, (tm, POOL_W), 1)
    half = jnp.left_shift(1, lane // POOL_GD)
    cnt = (jnp.minimum(t + half, rows) - jnp.maximum(t - half, 0)).astype(F32)

    def sh(k):
        return buf[pl.ds(HALO + k, tm), :]

    acc = sh(-1) + v
    pooled = jnp.where(lane < POOL_GD, acc, 0.0)
    lo, hi = -1, 0
    for g in range(1, len(POOL_WINDOWS)):
        hw = POOL_WINDOWS[g] // 2
        for k in list(range(-hw, lo)) + list(range(hi + 1, hw)):
            acc = acc + sh(k)
        lo, hi = -hw, hw - 1
        pooled = jnp.where(lane // POOL_GD == g, acc, pooled)
    pooled = pooled / cnt - v
    o_ref[0] = _dot(pooled.astype(BF16), w_ref[0]) * s_ref[0]


def _pool_call(l, pv, w_bd, scale):
    B, rows, _ = pv.shape
    tm = _tile(rows)
    prev, nxt = _halo_specs(tm, rows, POOL_W)
    return pl.pallas_call(
        functools.partial(_pool_kernel, tm=tm, rows=rows),
        grid=(B, rows // tm),
        in_specs=[
            pl.BlockSpec((1, tm, POOL_W), lambda b, i: (b, i, 0)), prev, nxt,
            pl.BlockSpec((1, POOL_W, POOL_W), lambda b, i: (l, 0, 0)),
            pl.BlockSpec((1, 1, POOL_W), lambda b, i: (l, 0, 0)),
        ],
        out_specs=pl.BlockSpec((1, tm, POOL_W), lambda b, i: (b, i, 0)),
        out_shape=jax.ShapeDtypeStruct((B, rows, POOL_W), F32),
        scratch_shapes=[pltpu.VMEM((tm + 2 * HALO, POOL_W), F32)],
        compiler_params=_params(("parallel", "parallel")),
        name="pool",
    )(pv, pv, pv, w_bd, scale)


CPT = TM_GDN // CHUNK
N_CHAIN = 2 * GDN_H


def _bdot(a, b):
    return lax.dot_general(a.astype(BF16), b.astype(BF16), (((2,), (1,)), ((0,), (0,))),
                           preferred_element_type=F32)


def _bdot_nt(a, b):
    return lax.dot_general(a.astype(BF16), b.astype(BF16), (((2,), (2,)), ((0,), (0,))),
                           preferred_element_type=F32)


def _split3(x):
    hi = x.astype(BF16)
    r1 = x - hi.astype(F32)
    mid = r1.astype(BF16)
    lo = (r1 - mid.astype(F32)).astype(BF16)
    return hi, mid, lo


def _gdn_prep_kernel(*refs, n_tiles, rope):
    if rope:
        (cur_ref, prev_ref, next_ref, gab_ref, cw_ref, alog_ref, dtb_ref, cos_ref, sin_ref,
         uf_o, ub_o, mf_o, mb_o, inf_o, inb_o, gl_o, buf, gt_buf) = refs
    else:
        (cur_ref, prev_ref, next_ref, gab_ref, cw_ref, alog_ref, dtb_ref,
         uf_o, ub_o, mf_o, mb_o, inf_o, inb_o, gl_o, buf, gt_buf) = refs
    tm = TM_GDN
    i = pl.program_id(1)
    _fill_halo_buf(buf, prev_ref, cur_ref, next_ref, i, n_tiles, tm)
    acc = buf[pl.ds(HALO - SHORT_CONV // 2, tm), :] * cw_ref[0, 0:1, :]
    for k in range(1, SHORT_CONV):
        acc = acc + buf[pl.ds(HALO + k - SHORT_CONV // 2, tm), :] * cw_ref[0, k:k + 1, :]
    qkv = _silu(acc)

    lane = lax.broadcasted_iota(jnp.int32, (tm, GDN_DH), 1)
    first = (lane % (GDN_DH // 2)) < (GDN_DH // 4)

    def norm_rope(xv):
        xn = xv * lax.rsqrt(jnp.sum(xv * xv, axis=-1, keepdims=True) + EPS)
        if not rope:
            return xn
        partner = jnp.where(first, pltpu.roll(xn, GDN_DH - GDN_DH // 4, axis=1),
                            pltpu.roll(xn, GDN_DH // 4, axis=1))
        return xn * cos_ref[...] + partner * sin_ref[...]

    gab = gab_ref[0]
    g_all = -jnp.exp(alog_ref[0]) * jax.nn.softplus(gab + dtb_ref[0])
    beta_all = jax.nn.sigmoid(gab)
    ri = lax.broadcasted_iota(jnp.int32, (tm, tm), 0)
    rj = lax.broadcasted_iota(jnp.int32, (tm, tm), 1)
    same = (ri // CHUNK) == (rj // CHUNK)
    low = jnp.logical_and(same, rj <= ri).astype(BF16)
    upp = jnp.logical_and(same, rj >= ri).astype(BF16)
    g_hi, g_mid, g_lo = _split3(g_all)
    pre = _dot(low, g_hi) + _dot(low, g_mid) + _dot(low, g_lo)
    suf = _dot(upp, g_hi) + _dot(upp, g_mid) + _dot(upp, g_lo)
    glane = lax.broadcasted_iota(jnp.int32, (tm, LANES), 1)
    gc_all = jnp.where(glane < GDN_H, pre, suf)
    gt_buf[...] = gc_all.T

    ci = lax.broadcasted_iota(jnp.int32, (CHUNK, CHUNK), 0)
    cj = lax.broadcasted_iota(jnp.int32, (CHUNK, CHUNK), 1)
    lhs_l, k_l, decay_l, strict_l, rhs_l, qd_l, kd_l, gl_l = [], [], [], [], [], [], [], []
    heads = []
    for h in range(GDN_H):
        q = norm_rope(qkv[:, h * GDN_DH:(h + 1) * GDN_DH]) * (GDN_DH ** -0.5)
        k = norm_rope(qkv[:, GDN_W + h * GDN_DH:GDN_W + (h + 1) * GDN_DH])
        v = qkv[:, 2 * GDN_W + h * GDN_DH:2 * GDN_W + (h + 1) * GDN_DH]
        heads.append((q.reshape(CPT, CHUNK, GDN_DH), k.reshape(CPT, CHUNK, GDN_DH),
                      v.reshape(CPT, CHUNK, GDN_DH)))
    for d in range(2):
        incl = (cj >= ci) if d else (cj <= ci)
        strict = (cj > ci) if d else (cj < ci)
        for h in range(GDN_H):
            q3, k3, v3 = heads[h]
            gi = d * GDN_H + h
            bi = 2 * GDN_H + gi
            gcol = gc_all[:, gi:gi + 1].reshape(CPT, CHUNK, 1)
            beta = beta_all[:, bi:bi + 1].reshape(CPT, CHUNK, 1)
            grow_full = gt_buf[gi:gi + 1, :]
            grow = jnp.stack([grow_full[:, c * CHUNK:(c + 1) * CHUNK] for c in range(CPT)], axis=0)
            diff = gcol - grow
            decay_l.append(jnp.where(incl, jnp.exp(jnp.where(incl, diff, 0.0)), 0.0))
            strict_l.append(jnp.broadcast_to(strict, (CPT, CHUNK, CHUNK)))
            lhs_l.append(jnp.concatenate([k3 * beta, q3], axis=1).astype(BF16))
            k_l.append(k3.astype(BF16))
            e = jnp.exp(gcol)
            rhs_l.append(jnp.concatenate([v3 * beta, k3 * (beta * e)], axis=2))
            g_last = gcol[:, 0:1, :] if d else gcol[:, CHUNK - 1:CHUNK, :]
            qd_l.append(q3 * e)
            kd_l.append(k3 * jnp.exp(g_last - gcol))
            gl_l.append(jnp.exp(g_last))

    decay = jnp.concatenate(decay_l, axis=0)
    strict = jnp.concatenate(strict_l, axis=0)
    m1 = _bdot_nt(jnp.concatenate(lhs_l, axis=0), jnp.concatenate(k_l, axis=0))
    a_mat = jnp.where(strict, m1[:, :CHUNK] * decay, 0.0)
    intra = m1[:, CHUNK:] * decay
    p = -a_mat
    t_off = p
    for _ in range(5):
        p = _bdot(p, p)
        t_off = t_off + p + _bdot(t_off, p)
    rhs = jnp.concatenate(rhs_l, axis=0)
    uw = rhs + _bdot(t_off, rhs)

    outs = ((uf_o, mf_o, inf_o), (ub_o, mb_o, inb_o))
    for d in range(2):
        u_o, m_o, in_o = outs[d]
        for h in range(GDN_H):
            n0 = (d * GDN_H + h) * CPT
            cols = slice(h * GDN_DH, (h + 1) * GDN_DH)
            u_o[0, :, cols] = uw[n0:n0 + CPT, :, :GDN_DH].reshape(tm, GDN_DH)
            m_o[0, 0, :, cols] = uw[n0:n0 + CPT, :, GDN_DH:].reshape(tm, GDN_DH).astype(BF16)
            m_o[0, 1, :, cols] = qd_l[d * GDN_H + h].reshape(tm, GDN_DH).astype(BF16)
            m_o[0, 2, :, cols] = kd_l[d * GDN_H + h].reshape(tm, GDN_DH).astype(BF16)
            in_o[0, h] = intra[n0:n0 + CPT].reshape(tm, CHUNK).astype(BF16)
            gl_o[0, h, :, d:d + 1, :] = jnp.broadcast_to(gl_l[d * GDN_H + h], (CPT, 1, LANES))


def _gdn_prep_call(l, gqkv, gab, conv_w, alog, dtb, rope_tables):
    B, rows, _ = gqkv.shape
    tm = TM_GDN
    n = rows // tm
    rope = rope_tables is not None
    prev, nxt = _halo_specs(tm, rows, 3 * GDN_W)

    def vec(width, sub=1):
        return pl.BlockSpec((1, sub, width), lambda b, i: (l, 0, 0))

    in_specs = [pl.BlockSpec((1, tm, 3 * GDN_W), lambda b, i: (b, i, 0)), prev, nxt,
                pl.BlockSpec((1, tm, LANES), lambda b, i: (b, i, 0)),
                vec(3 * GDN_W, SHORT_CONV), vec(LANES), vec(LANES)]
    args = [gqkv, gqkv, gqkv, gab, conv_w, alog, dtb]
    if rope:
        in_specs += [pl.BlockSpec((tm, GDN_DH), lambda b, i: (i, 0))] * 2
        args += list(rope_tables)
    u_spec = pl.BlockSpec((1, tm, GDN_W), lambda b, i: (b, i, 0))
    m_spec = pl.BlockSpec((1, 3, tm, GDN_W), lambda b, i: (b, 0, i, 0))
    in_spec = pl.BlockSpec((1, GDN_H, tm, CHUNK), lambda b, i: (b, 0, i, 0))
    gl_spec = pl.BlockSpec((1, GDN_H, CPT, 2, LANES), lambda b, i: (b, 0, i, 0, 0))
    u_sh = jax.ShapeDtypeStruct((B, rows, GDN_W), F32)
    m_sh = jax.ShapeDtypeStruct((B, 3, rows, GDN_W), BF16)
    in_sh = jax.ShapeDtypeStruct((B, GDN_H, rows, CHUNK), BF16)
    gl_sh = jax.ShapeDtypeStruct((B, GDN_H, rows // CHUNK, 2, LANES), F32)
    return pl.pallas_call(
        functools.partial(_gdn_prep_kernel, n_tiles=n, rope=rope),
        grid=(B, n),
        in_specs=in_specs,
        out_specs=[u_spec, u_spec, m_spec, m_spec, in_spec, in_spec, gl_spec],
        out_shape=[u_sh, u_sh, m_sh, m_sh, in_sh, in_sh, gl_sh],
        scratch_shapes=[pltpu.VMEM((tm + 2 * HALO, 3 * GDN_W), F32), pltpu.VMEM((LANES, tm), F32)],
        compiler_params=_params(("parallel", "parallel")),
        name="gdn_prep",
    )(*args)


def _gdn_scan_kernel(uf_ref, ub_ref, mf_ref, mb_ref, inf_ref, inb_ref, glf_ref, glb_ref, s0_ref,
                     of_o, ob_o, s_o, s_ref, *, n_tiles):
    i = pl.program_id(1)

    @pl.when(i == 0)
    def _():
        s_ref[...] = s0_ref[0]

    refs = ((uf_ref, mf_ref, inf_ref, glf_ref, of_o), (ub_ref, mb_ref, inb_ref, glb_ref, ob_o))
    states = [s_ref[n] for n in range(N_CHAIN)]
    for step in range(CPT):
        for d in range(2):
            u_ref, m_ref, in_ref, gl_ref, o_ref = refs[d]
            c = CPT - 1 - step if d else step
            rows = slice(c * CHUNK, (c + 1) * CHUNK)
            for h in range(GDN_H):
                cols = slice(h * GDN_DH, (h + 1) * GDN_DH)
                s = states[d * GDN_H + h]
                wq = jnp.concatenate([m_ref[0, 0, rows, cols], m_ref[0, 1, rows, cols]], axis=0)
                ws_qs = _dot(wq, s.astype(BF16))
                v_new = u_ref[0, rows, cols] - ws_qs[:CHUNK]
                vb = v_new.astype(BF16)
                o_ref[0, rows, cols] = ws_qs[CHUNK:] + _dot(in_ref[0, h, rows, :], vb)
                upd = lax.dot_general(m_ref[0, 2, rows, cols], vb, (((0,), (0,)), ((), ())),
                                      preferred_element_type=F32)
                states[d * GDN_H + h] = s * gl_ref[0, h, c, d:d + 1, :] + upd
    for n in range(N_CHAIN):
        s_ref[n] = states[n]

    @pl.when(i == n_tiles - 1)
    def _():
        s_o[0] = s_ref[...]


def _gdn_scan_call(prep, s0):
    uf, ub, mf, mb, inf, inb, gl = prep
    B, rows, _ = uf.shape
    tm = TM_GDN
    n = rows // tm

    def fwd(*blk):
        nd = len(blk)
        return pl.BlockSpec((1,) + blk, lambda b, i: (b,) + (0,) * (nd - 2) + (i, 0))

    def bwd(*blk):
        nd = len(blk)
        return pl.BlockSpec((1,) + blk, lambda b, i: (b,) + (0,) * (nd - 2) + (n - 1 - i, 0))

    gl_f = pl.BlockSpec((1, GDN_H, CPT, 2, LANES), lambda b, i: (b, 0, i, 0, 0))
    gl_b = pl.BlockSpec((1, GDN_H, CPT, 2, LANES), lambda b, i: (b, 0, n - 1 - i, 0, 0))
    s_spec = pl.BlockSpec((1, N_CHAIN, GDN_DH, GDN_DH), lambda b, i: (b, 0, 0, 0))
    o_shape = jax.ShapeDtypeStruct((B, rows, GDN_W), F32)
    return pl.pallas_call(
        functools.partial(_gdn_scan_kernel, n_tiles=n),
        grid=(B, n),
        in_specs=[fwd(tm, GDN_W), bwd(tm, GDN_W), fwd(3, tm, GDN_W), bwd(3, tm, GDN_W),
                  fwd(GDN_H, tm, CHUNK), bwd(GDN_H, tm, CHUNK), gl_f, gl_b, s_spec],
        out_specs=[fwd(tm, GDN_W), bwd(tm, GDN_W), s_spec],
        out_shape=[o_shape, o_shape, jax.ShapeDtypeStruct(s0.shape, F32)],
        scratch_shapes=[pltpu.VMEM((N_CHAIN, GDN_DH, GDN_DH), F32)],
        compiler_params=_params(("parallel", "arbitrary")),
        name="gdn_scan",
    )(uf, ub, mf, mb, inf, inb, gl, gl, s0)


NA_ROWS = SEQ // GRID_W
NA_BAND = NA_WIN_ROWS * GRID_W
_NT = (((1,), (1,)), ((), ()))


def _na_kernel(q_ref, k_ref, v_ref, kc_ref, vc_ref, bias_ref, o_ref):
    r = pl.program_id(1)
    q = q_ref[0]
    kc = kc_ref[0]
    vc = vc_ref[0]
    r0 = jnp.clip(r - NA_WIN_ROWS // 2, 0, NA_ROWS - NA_WIN_ROWS)
    d0 = r0 - r + NA_WIN_ROWS - 1
    start = pl.multiple_of(r0 * GRID_W, GRID_W)
    kb = k_ref[0, pl.ds(start, NA_BAND), :]
    vb = v_ref[0, pl.ds(start, NA_BAND), :]
    outs = []
    for h in range(NA_H):
        cols = slice(h * NA_DH, (h + 1) * NA_DH)
        qh = q[:, cols]
        s_loc = lax.dot_general(qh, kb[:, cols], _NT, preferred_element_type=F32) + bias_ref[h, d0]
        s_ctx = lax.dot_general(qh, kc[:, cols], _NT, preferred_element_type=F32)
        m = jnp.maximum(jnp.max(s_loc, axis=-1, keepdims=True),
                        jnp.max(s_ctx, axis=-1, keepdims=True))
        p_loc = jnp.exp(s_loc - m)
        p_ctx = jnp.exp(s_ctx - m)
        den = jnp.sum(p_loc, axis=-1, keepdims=True) + jnp.sum(p_ctx, axis=-1, keepdims=True)
        o = _dot(p_loc.astype(BF16), vb[:, cols]) + _dot(p_ctx.astype(BF16), vc[:, cols])
        outs.append(o / den)
    o_ref[0] = jnp.concatenate(outs, axis=-1)


def _na_call(q, k, v, kc, vc, bias):
    B = q.shape[0]
    full = pl.BlockSpec((1, SEQ, NA_W), lambda b, r: (b, 0, 0))
    cfull = pl.BlockSpec((1, CTX, NA_W), lambda b, r: (b, 0, 0))
    blk = pl.BlockSpec((1, GRID_W, NA_W), lambda b, r: (b, r, 0))
    return pl.pallas_call(
        _na_kernel,
        grid=(B, NA_ROWS),
        in_specs=[blk, full, full, cfull, cfull,
                  pl.BlockSpec(bias.shape, lambda b, r: (0, 0, 0, 0))],
        out_specs=blk,
        out_shape=jax.ShapeDtypeStruct((B, SEQ, NA_W), F32),
        compiler_params=_params(("parallel", "arbitrary")),
        name="natten",
    )(q, k, v, kc, vc, bias)


def _ctx_attn_kernel(q_ref, k_ref, v_ref, o_ref):
    q, k, v = q_ref[0], k_ref[0], v_ref[0]
    outs = []
    for h in range(NA_H):
        cols = slice(h * NA_DH, (h + 1) * NA_DH)
        s = lax.dot_general(q[:, cols], k[:, cols], _NT, preferred_element_type=F32)
        p = jnp.exp(s - jnp.max(s, axis=-1, keepdims=True))
        den = jnp.sum(p, axis=-1, keepdims=True)
        outs.append(_dot(p.astype(BF16), v[:, cols]) / den)
    o_ref[0] = jnp.concatenate(outs, axis=-1)


def _ctx_attn_call(q, k, v):
    B = q.shape[0]
    spec = pl.BlockSpec((1, CTX, NA_W), lambda b: (b, 0, 0))
    return pl.pallas_call(
        _ctx_attn_kernel,
        grid=(B,),
        in_specs=[spec, spec, spec],
        out_specs=spec,
        out_shape=jax.ShapeDtypeStruct((B, CTX, NA_W), F32),
        compiler_params=_params(("parallel",)),
        name="ctx_attn",
    )(q, k, v)


def _na_bias_table(rpb):
    qc = np.arange(GRID_W)[:, None]
    kc = np.arange(GRID_W)[None, :]
    c0 = np.clip(qc - NA_WIN_COLS // 2, 0, GRID_W - NA_WIN_COLS)
    mask = (kc >= c0) & (kc < c0 + NA_WIN_COLS)
    dc = np.clip(kc - qc + NA_WIN_COLS - 1, 0, 2 * NA_WIN_COLS - 2)
    dr = np.arange(NA_WIN_ROWS)[:, None] + np.arange(NA_WIN_ROWS)[None, :]
    t = rpb[:, dr][:, :, :, dc]
    t = jnp.where(mask[None, None, None], t, NEG)
    return t.transpose(0, 1, 3, 2, 4).reshape(NA_H, NA_WIN_ROWS, GRID_W, NA_BAND)


def _outproj_kernel(x_ref, ya_ref, of_ref, ob_ref, z_ref, yc_ref, g_ref, gn_ref, w_ref, o_ref):
    acc = _dot(ya_ref[0].astype(BF16), w_ref[0, 0:POOL_W, :])
    o = of_ref[0] + ob_ref[0]
    z = z_ref[0]
    for h in range(GDN_H):
        cols = slice(h * GDN_DH, (h + 1) * GDN_DH)
        oh = o[:, cols]
        on = oh * lax.rsqrt(jnp.mean(oh * oh, axis=-1, keepdims=True) + EPS) * gn_ref[0]
        yb = (on * _silu(z[:, cols])).astype(BF16)
        acc = acc + _dot(yb, w_ref[0, POOL_W + h * GDN_DH:POOL_W + (h + 1) * GDN_DH, :])
    acc = acc + _dot(yc_ref[0].astype(BF16), w_ref[0, POOL_W + GDN_W:, :])
    o_ref[0] = x_ref[0] + g_ref[0] * acc


def _outproj_call(l, xs, mod, mod_row, ya, o_f, o_b, gz, yc, gdn_norm, w_out):
    B, rows, _ = xs.shape
    tm = _tile(rows)

    def rowspec(n):
        return pl.BlockSpec((1, tm, n), lambda b, i: (b, i, 0))

    return pl.pallas_call(
        _outproj_kernel,
        grid=(B, rows // tm),
        in_specs=[rowspec(D_MODEL), rowspec(POOL_W), rowspec(GDN_W), rowspec(GDN_W), rowspec(GDN_W),
                  rowspec(NA_W), _mod_spec(2, mod_row),
                  pl.BlockSpec((1, 1, GDN_DH), lambda b, i: (l, 0, 0)),
                  pl.BlockSpec((1, D_MODEL, D_MODEL), lambda b, i: (l, 0, 0), **_SINGLE)],
        out_specs=rowspec(D_MODEL),
        out_shape=jax.ShapeDtypeStruct(xs.shape, F32),
        input_output_aliases={0: 0},
        compiler_params=_params(("parallel", "parallel")),
        name="outproj",
    )(xs, ya, o_f, o_b, gz, yc, mod, gdn_norm, w_out)


def _ffn_kernel(cur_ref, prev_ref, next_ref, sh_ref, sc_ref, g_ref, nw_ref, wu_ref, cw_ref, wd_ref,
                o_ref, xbuf, hbuf, ua_buf, ub_buf, acc, *, tm, n_tiles):
    i = pl.program_id(1)
    _fill_halo_buf(xbuf, prev_ref, cur_ref, next_ref, i, n_tiles, tm)
    x = xbuf[...]
    r = lax.rsqrt(jnp.mean(x * x, axis=-1, keepdims=True) + EPS)
    h = (x * r * nw_ref[0]) * (1.0 + sc_ref[0]) + sh_ref[0]
    rows = lax.broadcasted_iota(jnp.int32, (tm + 2 * HALO, 1), 0)
    ok = jnp.logical_and(jnp.logical_or(rows >= HALO, i > 0),
                         jnp.logical_or(rows < HALO + tm, i < n_tiles - 1))
    hbuf[...] = jnp.where(ok, h, 0.0).astype(BF16)
    acc[...] = jnp.zeros_like(acc)
    for j in range(D_FF // FF_TILE):
        ca = slice(j * FF_TILE, (j + 1) * FF_TILE)
        cb = slice(D_FF + j * FF_TILE, D_FF + (j + 1) * FF_TILE)
        hb = hbuf[...]
        ua_buf[...] = _dot(hb, wu_ref[0, :, ca])
        ub_buf[...] = _dot(hb, wu_ref[0, :, cb])
        a = jnp.zeros((tm, FF_TILE), F32)
        b = jnp.zeros((tm, FF_TILE), F32)
        for k in range(FFN_CONV):
            off = HALO + k - FFN_CONV // 2
            a = a + ua_buf[pl.ds(off, tm), :] * cw_ref[0, k:k + 1, ca]
            b = b + ub_buf[pl.ds(off, tm), :] * cw_ref[0, k:k + 1, cb]
        act = (_silu(a) * b).astype(BF16)
        acc[...] += _dot(act, wd_ref[0, ca, :])
    o_ref[0] = cur_ref[0] + g_ref[0] * acc[...]


def _ffn_call(l, xs, mod, mod_row, norm_ffn, w_up, ffn_conv, w_down):
    B, rows, _ = xs.shape
    tm = _tile(rows)
    n = rows // tm
    prev, nxt = _halo_specs(tm, rows, D_MODEL)
    return pl.pallas_call(
        functools.partial(_ffn_kernel, tm=tm, n_tiles=n),
        grid=(B, n),
        in_specs=[pl.BlockSpec((1, tm, D_MODEL), lambda b, i: (b, i, 0)), prev, nxt,
                  _mod_spec(3, mod_row), _mod_spec(4, mod_row), _mod_spec(5, mod_row),
                  pl.BlockSpec((1, 1, D_MODEL), lambda b, i: (l, 0, 0)),
                  pl.BlockSpec((1, D_MODEL, 2 * D_FF), lambda b, i: (l, 0, 0), **_SINGLE),
                  pl.BlockSpec((1, FFN_CONV, 2 * D_FF), lambda b, i: (l, 0, 0)),
                  pl.BlockSpec((1, D_FF, D_MODEL), lambda b, i: (l, 0, 0), **_SINGLE)],
        out_specs=pl.BlockSpec((1, tm, D_MODEL), lambda b, i: (b, i, 0)),
        out_shape=jax.ShapeDtypeStruct(xs.shape, F32),
        scratch_shapes=[pltpu.VMEM((tm + 2 * HALO, D_MODEL), F32),
                        pltpu.VMEM((tm + 2 * HALO, D_MODEL), BF16),
                        pltpu.VMEM((tm + 2 * HALO, FF_TILE), F32),
                        pltpu.VMEM((tm + 2 * HALO, FF_TILE), F32),
                        pltpu.VMEM((tm, D_MODEL), F32)],
        compiler_params=_params(("parallel", "parallel")),
        name="convffn",
    )(xs, xs, xs, mod, mod, mod, norm_ffn, w_up, ffn_conv, w_down)


def _rope_tables():
    half = GDN_DH // 2
    nf = half // 2
    t = jnp.arange(SEQ)
    freqs = ROPE_THETA ** (-jnp.arange(nf, dtype=F32) / nf)

    def part(pos):
        ang = pos.astype(F32)[:, None] * freqs
        cos, sin = jnp.cos(ang), jnp.sin(ang)
        return jnp.concatenate([cos, cos], axis=-1), jnp.concatenate([-sin, sin], axis=-1)

    c_r, s_r = part(t // GRID_W)
    c_c, s_c = part(t % GRID_W)
    return jnp.concatenate([c_r, c_c], axis=-1), jnp.concatenate([s_r, s_c], axis=-1)


def _pad_lanes(a, n=LANES):
    return jnp.pad(a, [(0, 0)] * (a.ndim - 1) + [(0, n - a.shape[-1])])


def kernel(x, c, ctx, c_ctx, w_ada, b_ada, norm_mix, w_in, pool_w, pool_scale, gdn_conv, gdn_a_log,
           gdn_dt_bias, gdn_norm, na_q_norm, na_k_norm, na_rpb, w_out, norm_ffn, w_up, ffn_conv, w_down):
    B = x.shape[0]
    depth = w_ada.shape[0]

    rows = -(-(B + 1) // SUBLANES) * SUBLANES
    cvec = jnp.concatenate([c, c_ctx[None, :], jnp.zeros((rows - B - 1, D_MODEL), F32)], axis=0)
    w_in_b = w_in.astype(BF16)
    wts = (w_in_b[:, :, :POOL_END], w_in_b[:, :, POOL_END:GQKV_END], w_in_b[:, :, GQKV_END:GZ_END],
           _pad_lanes(w_in_b[:, :, GZ_END:GAB_END]), w_in_b[:, :, GAB_END:])
    eye_g = jnp.eye(len(POOL_WINDOWS), dtype=F32)
    pool_bd = jnp.einsum("lgcd,gh->lgchd", pool_w, eye_g).reshape(depth, POOL_W, POOL_W).astype(BF16)
    alog = _pad_lanes(gdn_a_log.reshape(depth, 1, 2 * GDN_H))
    dtb = _pad_lanes(gdn_dt_bias.reshape(depth, 1, 2 * GDN_H))
    rope_tables = _rope_tables()
    qn_t = jnp.tile(na_q_norm, (1, NA_H)).reshape(depth, 1, NA_W)
    kn_t = jnp.tile(na_k_norm, (1, NA_H)).reshape(depth, 1, NA_W)
    w_out_b = w_out.astype(BF16)
    w_up_b = w_up.astype(BF16)
    w_down_b = w_down.astype(BF16)
    norm_mix3 = norm_mix.reshape(depth, 1, D_MODEL)
    norm_ffn3 = norm_ffn.reshape(depth, 1, D_MODEL)
    pool_scale3 = pool_scale.reshape(depth, 1, POOL_W)
    gdn_norm3 = gdn_norm.reshape(depth, 1, GDN_DH)
    s_zero = jnp.zeros((B, N_CHAIN, GDN_DH, GDN_DH), F32)

    mods = _ada_call(cvec, w_ada, b_ada).reshape(depth, rows, 1, 6 * D_MODEL)

    for l in range(depth):
        mod = mods[l]
        with_ctx_out = l < depth - 1
        pv, gqkv, gz, gab, naq, nak, nav = _inproj_call(l, x, mod, None, norm_mix3, wts, qn_t, kn_t)
        cpv, cgqkv, cgz, cgab, cnaq, cnak, cnav = _inproj_call(l, ctx, mod, B, norm_mix3, wts, qn_t, kn_t)
        cprep = _gdn_prep_call(l, cgqkv, cgab, gdn_conv, alog, dtb, None)
        co_f, co_b, s_ctx = _gdn_scan_call(cprep, s_zero)
        prep = _gdn_prep_call(l, gqkv, gab, gdn_conv, alog, dtb, rope_tables)
        o_f, o_b, _ = _gdn_scan_call(prep, s_ctx)
        ya = _pool_call(l, pv, pool_bd, pool_scale3)
        yc = _na_call(naq, nak, nav, cnak, cnav, _na_bias_table(na_rpb[l]))
        x = _outproj_call(l, x, mod, None, ya, o_f, o_b, gz, yc, gdn_norm3, w_out_b)
        x = _ffn_call(l, x, mod, None, norm_ffn3, w_up_b, ffn_conv, w_down_b)
        if with_ctx_out:
            cya = _pool_call(l, cpv, pool_bd, pool_scale3)
            cyc = _ctx_attn_call(cnaq, cnak, cnav)
            ctx = _outproj_call(l, ctx, mod, B, cya, co_f, co_b, cgz, cyc, gdn_norm3, w_out_b)
            ctx = _ffn_call(l, ctx, mod, B, norm_ffn3, w_up_b, ffn_conv, w_down_b)
    return x
```

```python
import functools

import numpy as np
import jax
import jax.numpy as jnp
from jax import lax
from jax.experimental import pallas as pl
from jax.experimental.pallas import tpu as pltpu

F32 = jnp.float32
BF16 = jnp.bfloat16

D_MODEL = 1024
SEQ = 4096
CTX = 256
GRID_W = 64
EPS = 1e-6

POOL_W = 256
POOL_WINDOWS = (2, 4, 8, 16)
POOL_GD = 64
GDN_W = 512
GDN_H = 4
GDN_DH = 128
CHUNK = 64
SHORT_CONV = 5
ROPE_THETA = 10000.0
NA_W = 256
NA_H = 4
NA_DH = 64
NA_WIN_ROWS = 8
NA_WIN_COLS = 16
D_FF = 2816
FFN_CONV = 3

POOL_END = POOL_W
GQKV_END = POOL_END + 3 * GDN_W
GZ_END = GQKV_END + GDN_W
GAB_END = GZ_END + 4 * GDN_H

LANES = 128
SUBLANES = 8
HALO = SUBLANES
TM_MM = 512
TM_GDN = 256
FF_TILE = 256
NEG = -1e30
VMEM_LIMIT = 56 * 1024 * 1024

_HI = lax.Precision.HIGHEST
_SINGLE = dict(pipeline_mode=pl.Buffered(1))


def _dot(a, b):
    return jnp.dot(a, b, preferred_element_type=F32)


def _dot_hi(a, b):
    return jnp.dot(a, b, preferred_element_type=F32, precision=_HI)


def _silu(x):
    return x * jax.nn.sigmoid(x)


def _params(sem, vmem=VMEM_LIMIT):
    return pltpu.CompilerParams(dimension_semantics=sem, vmem_limit_bytes=vmem)


def _tile(rows):
    return min(rows, TM_MM)


def _mod_spec(chunk, row):
    return pl.BlockSpec((1, 1, D_MODEL), lambda b, i: (b if row is None else row, 0, chunk))


def _halo_specs(tm, rows, width, col=None):
    nb = tm // HALO
    last = rows // HALO - 1
    c = 0 if col is None else col
    prev = pl.BlockSpec((1, HALO, width), lambda b, i: (b, jnp.maximum(i * nb - 1, 0), c))
    nxt = pl.BlockSpec((1, HALO, width), lambda b, i: (b, jnp.minimum((i + 1) * nb, last), c))
    return prev, nxt


def _fill_halo_buf(buf, prev_ref, cur_ref, next_ref, i, n_tiles, tm):
    buf[0:HALO, :] = jnp.where(i > 0, prev_ref[0], 0.0)
    buf[HALO:HALO + tm, :] = cur_ref[0]
    buf[HALO + tm:2 * HALO + tm, :] = jnp.where(i < n_tiles - 1, next_ref[0], 0.0)


def _ada_kernel(c_ref, w_ref, b_ref, o_ref):
    sc = _silu(c_ref[...]).astype(BF16)
    o_ref[0] = _dot(sc, w_ref[0].astype(BF16)) + b_ref[0]


def _ada_call(cvec, w_ada, b_ada):
    depth = w_ada.shape[0]
    rows = cvec.shape[0]
    tn = 1536
    return pl.pallas_call(
        _ada_kernel,
        grid=(depth, 6 * D_MODEL // tn),
        in_specs=[
            pl.BlockSpec((rows, D_MODEL), lambda l, j: (0, 0)),
            pl.BlockSpec((1, D_MODEL, tn), lambda l, j: (l, 0, j)),
            pl.BlockSpec((1, 1, tn), lambda l, j: (l, 0, j)),
        ],
        out_specs=pl.BlockSpec((1, rows, tn), lambda l, j: (l, 0, j)),
        out_shape=jax.ShapeDtypeStruct((depth, rows, 6 * D_MODEL), F32),
        compiler_params=_params(("parallel", "parallel")),
        name="adaln",
    )(cvec, w_ada, b_ada.reshape(depth, 1, 6 * D_MODEL))


def _inproj_kernel(x_ref, sh_ref, sc_ref, nw_ref, wp_ref, wq_ref, wz_ref, wab_ref, wna_ref,
                   qn_ref, kn_ref, pool_o, gqkv_o, gz_o, gab_o, naq_o, nak_o, nav_o):
    x = x_ref[0]
    r = lax.rsqrt(jnp.mean(x * x, axis=-1, keepdims=True) + EPS)
    h = (x * r * nw_ref[0]) * (1.0 + sc_ref[0]) + sh_ref[0]
    hb = h.astype(BF16)
    pool_o[0] = _dot(hb, wp_ref[0])
    gqkv_o[0] = _dot(hb, wq_ref[0])
    gz_o[0] = _dot(hb, wz_ref[0])
    gab_o[0] = _dot(hb, wab_ref[0])
    na = _dot(hb, wna_ref[0])
    li = lax.broadcasted_iota(jnp.int32, (NA_W, NA_W), 0) // NA_DH
    lj = lax.broadcasted_iota(jnp.int32, (NA_W, NA_W), 1) // NA_DH
    seg = (li == lj).astype(BF16)
    q = na[:, 0:NA_W]
    k = na[:, NA_W:2 * NA_W]

    def seg_mean(sq):
        hi = sq.astype(BF16)
        lo = (sq - hi.astype(F32)).astype(BF16)
        return (_dot(hi, seg) + _dot(lo, seg)) * (1.0 / NA_DH)

    qs = seg_mean(q * q)
    ks = seg_mean(k * k)
    naq_o[0] = (q * lax.rsqrt(qs + EPS) * qn_ref[0] * (NA_DH ** -0.5)).astype(BF16)
    nak_o[0] = (k * lax.rsqrt(ks + EPS) * kn_ref[0]).astype(BF16)
    nav_o[0] = na[:, 2 * NA_W:3 * NA_W].astype(BF16)


def _inproj_call(l, xs, mod, mod_row, norm_mix, wts, qn_t, kn_t):
    B, rows, _ = xs.shape
    tm = _tile(rows)

    def wspec(w):
        return pl.BlockSpec((1,) + w.shape[1:], lambda b, i: (l, 0, 0), **_SINGLE)

    def rowspec(n):
        return pl.BlockSpec((1, tm, n), lambda b, i: (b, i, 0))

    def vec(n):
        return pl.BlockSpec((1, 1, n), lambda b, i: (l, 0, 0))

    outs = [(POOL_W, F32), (3 * GDN_W, F32), (GDN_W, F32), (LANES, F32),
            (NA_W, BF16), (NA_W, BF16), (NA_W, BF16)]
    return pl.pallas_call(
        _inproj_kernel,
        grid=(B, rows // tm),
        in_specs=[rowspec(D_MODEL), _mod_spec(0, mod_row), _mod_spec(1, mod_row), vec(D_MODEL)]
        + [wspec(w) for w in wts] + [vec(NA_W), vec(NA_W)],
        out_specs=[rowspec(n) for n, _ in outs],
        out_shape=[jax.ShapeDtypeStruct((B, rows, n), dt) for n, dt in outs],
        compiler_params=_params(("parallel", "parallel")),
        name="inproj",
    )(xs, mod, mod, norm_mix, *wts, qn_t, kn_t)


def _pool_kernel(cur_ref, prev_ref, next_ref, w_ref, s_ref, o_ref, buf, *, tm, rows):
    i = pl.program_id(1)
    _fill_halo_buf(buf, prev_ref, cur_ref, next_ref, i, rows // tm, tm)
    v = cur_ref[0]
    t = lax.broadcasted_iota(jnp.int32, (tm, POOL_W), 0) + i * tm
    lane = lax.broadcasted_iota(jnp.int32, (tm, POOL_W), 1)
    half = jnp.left_shift(1, lane // POOL_GD)
    cnt = (jnp.minimum(t + half, rows) - jnp.maximum(t - half, 0)).astype(F32)

    def sh(k):
        return buf[pl.ds(HALO + k, tm), :]

    acc = sh(-1) + v
    pooled = jnp.where(lane < POOL_GD, acc, 0.0)
    lo, hi = -1, 0
    for g in range(1, len(POOL_WINDOWS)):
        hw = POOL_WINDOWS[g] // 2
        for k in list(range(-hw, lo)) + list(range(hi + 1, hw)):
            acc = acc + sh(k)
        lo, hi = -hw, hw - 1
        pooled = jnp.where(lane // POOL_GD == g, acc, pooled)
    pooled = pooled / cnt - v
    o_ref[0] = _dot(pooled.astype(BF16), w_ref[0]) * s_ref[0]


def _pool_call(l, pv, w_bd, scale):
    B, rows, _ = pv.shape
    tm = _tile(rows)
    prev, nxt = _halo_specs(tm, rows, POOL_W)
    return pl.pallas_call(
        functools.partial(_pool_kernel, tm=tm, rows=rows),
        grid=(B, rows // tm),
        in_specs=[
            pl.BlockSpec((1, tm, POOL_W), lambda b, i: (b, i, 0)), prev, nxt,
            pl.BlockSpec((1, POOL_W, POOL_W), lambda b, i: (l, 0, 0)),
            pl.BlockSpec((1, 1, POOL_W), lambda b, i: (l, 0, 0)),
        ],
        out_specs=pl.BlockSpec((1, tm, POOL_W), lambda b, i: (b, i, 0)),
        out_shape=jax.ShapeDtypeStruct((B, rows, POOL_W), F32),
        scratch_shapes=[pltpu.VMEM((tm + 2 * HALO, POOL_W), F32)],
        compiler_params=_params(("parallel", "parallel")),
        name="pool",
    )(pv, pv, pv, w_bd, scale)


CPT = TM_GDN // CHUNK
N_CHAIN = 2 * GDN_H


def _bdot(a, b):
    return lax.dot_general(a.astype(BF16), b.astype(BF16), (((2,), (1,)), ((0,), (0,))),
                           preferred_element_type=F32)


def _bdot_nt(a, b):
    return lax.dot_general(a.astype(BF16), b.astype(BF16), (((2,), (2,)), ((0,), (0,))),
                           preferred_element_type=F32)


def _split3(x):
    hi = x.astype(BF16)
    r1 = x - hi.astype(F32)
    mid = r1.astype(BF16)
    lo = (r1 - mid.astype(F32)).astype(BF16)
    return hi, mid, lo


def _gdn_prep_kernel(*refs, n_tiles, rope):
    if rope:
        (cur_ref, prev_ref, next_ref, gab_ref, cw_ref, alog_ref, dtb_ref, cos_ref, sin_ref,
         uf_o, ub_o, wqf_o, wqb_o, ikf_o, ikb_o, gl_o, buf, gt_buf) = refs
    else:
        (cur_ref, prev_ref, next_ref, gab_ref, cw_ref, alog_ref, dtb_ref,
         uf_o, ub_o, wqf_o, wqb_o, ikf_o, ikb_o, gl_o, buf, gt_buf) = refs
    tm = TM_GDN
    i = pl.program_id(1)
    _fill_halo_buf(buf, prev_ref, cur_ref, next_ref, i, n_tiles, tm)
    xb = buf[...]
    nb = tm + 2 * HALO
    acc = None
    for k in range(SHORT_CONV):
        sh = (SHORT_CONV // 2 - k) % nb
        xs = xb if sh == 0 else pltpu.roll(xb, sh, axis=0)
        term = xs[HALO:HALO + tm] * cw_ref[0, k:k + 1, :]
        acc = term if acc is None else acc + term
    qkv = _silu(acc)

    lane = lax.broadcasted_iota(jnp.int32, (tm, GDN_DH), 1)
    first = (lane % (GDN_DH // 2)) < (GDN_DH // 4)

    def norm_rope(xv):
        xn = xv * lax.rsqrt(jnp.sum(xv * xv, axis=-1, keepdims=True) + EPS)
        if not rope:
            return xn
        partner = jnp.where(first, pltpu.roll(xn, GDN_DH - GDN_DH // 4, axis=1),
                            pltpu.roll(xn, GDN_DH // 4, axis=1))
        return xn * cos_ref[...] + partner * sin_ref[...]

    gab = gab_ref[0]
    g_all = -jnp.exp(alog_ref[0]) * jax.nn.softplus(gab + dtb_ref[0])
    beta_all = jax.nn.sigmoid(gab)
    ri = lax.broadcasted_iota(jnp.int32, (tm, tm), 0)
    rj = lax.broadcasted_iota(jnp.int32, (tm, tm), 1)
    same = (ri // CHUNK) == (rj // CHUNK)
    low = jnp.logical_and(same, rj <= ri).astype(BF16)
    upp = jnp.logical_and(same, rj >= ri).astype(BF16)
    g_hi, g_mid, g_lo = _split3(g_all)
    pre = _dot(low, g_hi) + _dot(low, g_mid) + _dot(low, g_lo)
    suf = _dot(upp, g_hi) + _dot(upp, g_mid) + _dot(upp, g_lo)
    glane = lax.broadcasted_iota(jnp.int32, (tm, LANES), 1)
    gc_all = jnp.where(glane < GDN_H, pre, suf)
    gt_buf[...] = gc_all.T

    ci = lax.broadcasted_iota(jnp.int32, (CHUNK, CHUNK), 0)
    cj = lax.broadcasted_iota(jnp.int32, (CHUNK, CHUNK), 1)
    lhs_l, k_l, decay_l, strict_l, rhs_l, qd_l, kd_l, gl_l = [], [], [], [], [], [], [], []
    heads = []
    for h in range(GDN_H):
        q = norm_rope(qkv[:, h * GDN_DH:(h + 1) * GDN_DH]) * (GDN_DH ** -0.5)
        k = norm_rope(qkv[:, GDN_W + h * GDN_DH:GDN_W + (h + 1) * GDN_DH])
        v = qkv[:, 2 * GDN_W + h * GDN_DH:2 * GDN_W + (h + 1) * GDN_DH]
        heads.append((q.reshape(CPT, CHUNK, GDN_DH), k.reshape(CPT, CHUNK, GDN_DH),
                      v.reshape(CPT, CHUNK, GDN_DH)))
    for d in range(2):
        incl = (cj >= ci) if d else (cj <= ci)
        strict = (cj > ci) if d else (cj < ci)
        for h in range(GDN_H):
            q3, k3, v3 = heads[h]
            gi = d * GDN_H + h
            bi = 2 * GDN_H + gi
            gcol = gc_all[:, gi:gi + 1].reshape(CPT, CHUNK, 1)
            beta = beta_all[:, bi:bi + 1].reshape(CPT, CHUNK, 1)
            grow_full = gt_buf[gi:gi + 1, :]
            grow = jnp.stack([grow_full[:, c * CHUNK:(c + 1) * CHUNK] for c in range(CPT)], axis=0)
            diff = gcol - grow
            decay_l.append(jnp.where(incl, jnp.exp(jnp.where(incl, diff, 0.0)), 0.0))
            strict_l.append(jnp.broadcast_to(strict, (CPT, CHUNK, CHUNK)))
            lhs_l.append(jnp.concatenate([k3 * beta, q3], axis=1).astype(BF16))
            k_l.append(k3.astype(BF16))
            e = jnp.exp(gcol)
            rhs_l.append(jnp.concatenate([v3 * beta, k3 * (beta * e)], axis=2))
            g_last = gcol[:, 0:1, :] if d else gcol[:, CHUNK - 1:CHUNK, :]
            qd_l.append(q3 * e)
            kd_l.append(k3 * jnp.exp(g_last - gcol))
            gl_l.append(jnp.exp(g_last))

    decay = jnp.concatenate(decay_l, axis=0)
    strict = jnp.concatenate(strict_l, axis=0)
    m1 = _bdot_nt(jnp.concatenate(lhs_l, axis=0), jnp.concatenate(k_l, axis=0))
    a_mat = jnp.where(strict, m1[:, :CHUNK] * decay, 0.0)
    intra = m1[:, CHUNK:] * decay
    p = -a_mat
    t_off = p
    for _ in range(5):
        p = _bdot(p, p)
        t_off = t_off + p + _bdot(t_off, p)
    rhs = jnp.concatenate(rhs_l, axis=0)
    uw = rhs + _bdot(t_off, rhs)

    outs = ((uf_o, wqf_o, ikf_o), (ub_o, wqb_o, ikb_o))
    for d in range(2):
        u_o, wq_o, ik_o = outs[d]
        for h in range(GDN_H):
            n0 = (d * GDN_H + h) * CPT
            cols = slice(h * GDN_DH, (h + 1) * GDN_DH)
            u_o[0, :, cols] = uw[n0:n0 + CPT, :, :GDN_DH].reshape(tm, GDN_DH)
            wq_o[0, h, :, 0:CHUNK, :] = uw[n0:n0 + CPT, :, GDN_DH:].astype(BF16)
            wq_o[0, h, :, CHUNK:, :] = qd_l[d * GDN_H + h].astype(BF16)
            ik_o[0, h, :, 0:CHUNK, :] = intra[n0:n0 + CPT].astype(BF16)
            kd3 = kd_l[d * GDN_H + h]
            for c in range(CPT):
                ik_o[0, h, c, CHUNK:, :] = kd3[c].T.astype(BF16)
            gl_o[0, h, :, d:d + 1, :] = jnp.broadcast_to(gl_l[d * GDN_H + h], (CPT, 1, LANES))


def _gdn_prep_call(l, gqkv, gab, conv_w, alog, dtb, rope_tables):
    B, rows, _ = gqkv.shape
    tm = TM_GDN
    n = rows // tm
    rope = rope_tables is not None
    prev, nxt = _halo_specs(tm, rows, 3 * GDN_W)

    def vec(width, sub=1):
        return pl.BlockSpec((1, sub, width), lambda b, i: (l, 0, 0))

    in_specs = [pl.BlockSpec((1, tm, 3 * GDN_W), lambda b, i: (b, i, 0)), prev, nxt,
                pl.BlockSpec((1, tm, LANES), lambda b, i: (b, i, 0)),
                vec(3 * GDN_W, SHORT_CONV), vec(LANES), vec(LANES)]
    args = [gqkv, gqkv, gqkv, gab, conv_w, alog, dtb]
    if rope:
        in_specs += [pl.BlockSpec((tm, GDN_DH), lambda b, i: (i, 0))] * 2
        args += list(rope_tables)
    nc = rows // CHUNK
    u_spec = pl.BlockSpec((1, tm, GDN_W), lambda b, i: (b, i, 0))
    wq_spec = pl.BlockSpec((1, GDN_H, CPT, 2 * CHUNK, GDN_DH), lambda b, i: (b, 0, i, 0, 0))
    ik_spec = pl.BlockSpec((1, GDN_H, CPT, 3 * CHUNK, CHUNK), lambda b, i: (b, 0, i, 0, 0))
    gl_spec = pl.BlockSpec((1, GDN_H, CPT, 2, LANES), lambda b, i: (b, 0, i, 0, 0))
    u_sh = jax.ShapeDtypeStruct((B, rows, GDN_W), F32)
    wq_sh = jax.ShapeDtypeStruct((B, GDN_H, nc, 2 * CHUNK, GDN_DH), BF16)
    ik_sh = jax.ShapeDtypeStruct((B, GDN_H, nc, 3 * CHUNK, CHUNK), BF16)
    gl_sh = jax.ShapeDtypeStruct((B, GDN_H, nc, 2, LANES), F32)
    return pl.pallas_call(
        functools.partial(_gdn_prep_kernel, n_tiles=n, rope=rope),
        grid=(B, n),
        in_specs=in_specs,
        out_specs=[u_spec, u_spec, wq_spec, wq_spec, ik_spec, ik_spec, gl_spec],
        out_shape=[u_sh, u_sh, wq_sh, wq_sh, ik_sh, ik_sh, gl_sh],
        scratch_shapes=[pltpu.VMEM((tm + 2 * HALO, 3 * GDN_W), F32), pltpu.VMEM((LANES, tm), F32)],
        compiler_params=_params(("parallel", "parallel")),
        name="gdn_prep",
    )(*args)


def _gdn_scan_kernel(uf_ref, ub_ref, wqf_ref, wqb_ref, ikf_ref, ikb_ref, glf_ref, glb_ref, s0_ref,
                     of_o, ob_o, s_o, s_ref, *, n_tiles):
    i = pl.program_id(1)

    @pl.when(i == 0)
    def _():
        s_ref[...] = s0_ref[0]

    refs = ((uf_ref, wqf_ref, ikf_ref, glf_ref, of_o), (ub_ref, wqb_ref, ikb_ref, glb_ref, ob_o))
    chains = [(d, h) for d in range(2) for h in range(GDN_H)]
    states = [s_ref[n] for n in range(N_CHAIN)]
    for step in range(CPT):
        ws_qs = []
        for n, (d, h) in enumerate(chains):
            c = CPT - 1 - step if d else step
            ws_qs.append(_dot(refs[d][1][0, h, c], states[n].astype(BF16)))
        vb = []
        for n, (d, h) in enumerate(chains):
            c = CPT - 1 - step if d else step
            u = refs[d][0][0, c * CHUNK:(c + 1) * CHUNK, h * GDN_DH:(h + 1) * GDN_DH]
            vb.append((u - ws_qs[n][:CHUNK]).astype(BF16))
        r = []
        for n, (d, h) in enumerate(chains):
            c = CPT - 1 - step if d else step
            r.append(_dot(refs[d][2][0, h, c], vb[n]))
        for n, (d, h) in enumerate(chains):
            c = CPT - 1 - step if d else step
            refs[d][4][0, c * CHUNK:(c + 1) * CHUNK, h * GDN_DH:(h + 1) * GDN_DH] = (
                ws_qs[n][CHUNK:] + r[n][:CHUNK])
            states[n] = states[n] * refs[d][3][0, h, c, d:d + 1, :] + r[n][CHUNK:]
    for n in range(N_CHAIN):
        s_ref[n] = states[n]

    @pl.when(i == n_tiles - 1)
    def _():
        s_o[0] = s_ref[...]


def _gdn_scan_call(prep, s0):
    uf, ub, wqf, wqb, ikf, ikb, gl = prep
    B, rows, _ = uf.shape
    tm = TM_GDN
    n = rows // tm

    def ident(i):
        return i

    def rev(i):
        return n - 1 - i

    def rowspec(order):
        return pl.BlockSpec((1, tm, GDN_W), lambda b, i: (b, order(i), 0))

    def tilespec(arr, order):
        return pl.BlockSpec((1, GDN_H, CPT) + arr.shape[3:], lambda b, i: (b, 0, order(i), 0, 0))

    s_spec = pl.BlockSpec((1, N_CHAIN, GDN_DH, GDN_DH), lambda b, i: (b, 0, 0, 0))
    o_shape = jax.ShapeDtypeStruct((B, rows, GDN_W), F32)
    return pl.pallas_call(
        functools.partial(_gdn_scan_kernel, n_tiles=n),
        grid=(B, n),
        in_specs=[rowspec(ident), rowspec(rev), tilespec(wqf, ident), tilespec(wqb, rev),
                  tilespec(ikf, ident), tilespec(ikb, rev), tilespec(gl, ident), tilespec(gl, rev),
                  s_spec],
        out_specs=[rowspec(ident), rowspec(rev), s_spec],
        out_shape=[o_shape, o_shape, jax.ShapeDtypeStruct(s0.shape, F32)],
        scratch_shapes=[pltpu.VMEM((N_CHAIN, GDN_DH, GDN_DH), F32)],
        compiler_params=_params(("parallel", "arbitrary")),
        name="gdn_scan",
    )(uf, ub, wqf, wqb, ikf, ikb, gl, gl, s0)


NA_ROWS = SEQ // GRID_W
NA_BAND = NA_WIN_ROWS * GRID_W
NA_RPS = 4
_NT = (((1,), (1,)), ((), ()))


def _na_kernel(q_ref, k_ref, v_ref, kc_ref, vc_ref, bias_ref, o_ref):
    step = pl.program_id(1)
    q = q_ref[0]
    kc = kc_ref[0]
    vc = vc_ref[0]
    heads = [slice(h * NA_DH, (h + 1) * NA_DH) for h in range(NA_H)]
    qrows = [slice(j * GRID_W, (j + 1) * GRID_W) for j in range(NA_RPS)]
    kb, vb, d0 = [], [], []
    for j in range(NA_RPS):
        r = step * NA_RPS + j
        r0 = jnp.clip(r - NA_WIN_ROWS // 2, 0, NA_ROWS - NA_WIN_ROWS)
        d0.append(r0 - r + NA_WIN_ROWS - 1)
        start = pl.multiple_of(r0 * GRID_W, GRID_W)
        kb.append(k_ref[0, pl.ds(start, NA_BAND), :])
        vb.append(v_ref[0, pl.ds(start, NA_BAND), :])
    def scores(j):
        return [lax.dot_general(q[qrows[j], c], kb[j][:, c], _NT, preferred_element_type=F32)
                for c in heads]

    s_ctx = [lax.dot_general(q[:, c], kc[:, c], _NT, preferred_element_type=F32) for c in heads]
    p_ctx = [[None] * NA_RPS for _ in range(NA_H)]
    den = [[None] * NA_H for _ in range(NA_RPS)]
    o_loc = [None] * NA_RPS
    s_next = scores(0)
    for j in range(NA_RPS):
        s_loc = s_next
        if j + 1 < NA_RPS:
            s_next = scores(j + 1)
        p_loc = []
        for h in range(NA_H):
            sl = s_loc[h] + bias_ref[h, d0[j]]
            sc = s_ctx[h][qrows[j]]
            m = jnp.maximum(jnp.max(sl, axis=-1, keepdims=True), jnp.max(sc, axis=-1, keepdims=True))
            pl_ = jnp.exp(sl - m)
            pc_ = jnp.exp(sc - m)
            den[j][h] = jnp.sum(pl_, axis=-1, keepdims=True) + jnp.sum(pc_, axis=-1, keepdims=True)
            p_loc.append(pl_.astype(BF16))
            p_ctx[h][j] = pc_.astype(BF16)
        o_loc[j] = [_dot(p_loc[h], vb[j][:, c]) for h, c in enumerate(heads)]
    o_ctx = [_dot(jnp.concatenate(p_ctx[h], axis=0), vc[:, c]) for h, c in enumerate(heads)]
    for j in range(NA_RPS):
        o_ref[0, qrows[j], :] = jnp.concatenate(
            [(o_loc[j][h] + o_ctx[h][qrows[j]]) / den[j][h] for h in range(NA_H)], axis=-1)


def _na_call(l, q, k, v, kc, vc, bias):
    B = q.shape[0]
    full = pl.BlockSpec((1, SEQ, NA_W), lambda b, r: (b, 0, 0))
    cfull = pl.BlockSpec((1, CTX, NA_W), lambda b, r: (b, 0, 0))
    blk = pl.BlockSpec((1, NA_RPS * GRID_W, NA_W), lambda b, r: (b, r, 0))
    return pl.pallas_call(
        _na_kernel,
        grid=(B, NA_ROWS // NA_RPS),
        in_specs=[blk, full, full, cfull, cfull,
                  pl.BlockSpec((NA_H,) + bias.shape[1:], lambda b, r: (l, 0, 0, 0))],
        out_specs=blk,
        out_shape=jax.ShapeDtypeStruct((B, SEQ, NA_W), F32),
        compiler_params=_params(("parallel", "arbitrary")),
        name="natten",
    )(q, k, v, kc, vc, bias)


def _ctx_attn_kernel(q_ref, k_ref, v_ref, o_ref):
    q, k, v = q_ref[0], k_ref[0], v_ref[0]
    outs = []
    for h in range(NA_H):
        cols = slice(h * NA_DH, (h + 1) * NA_DH)
        s = lax.dot_general(q[:, cols], k[:, cols], _NT, preferred_element_type=F32)
        p = jnp.exp(s - jnp.max(s, axis=-1, keepdims=True))
        den = jnp.sum(p, axis=-1, keepdims=True)
        outs.append(_dot(p.astype(BF16), v[:, cols]) / den)
    o_ref[0] = jnp.concatenate(outs, axis=-1)


def _ctx_attn_call(q, k, v):
    B = q.shape[0]
    spec = pl.BlockSpec((1, CTX, NA_W), lambda b: (b, 0, 0))
    return pl.pallas_call(
        _ctx_attn_kernel,
        grid=(B,),
        in_specs=[spec, spec, spec],
        out_specs=spec,
        out_shape=jax.ShapeDtypeStruct((B, CTX, NA_W), F32),
        compiler_params=_params(("parallel",)),
        name="ctx_attn",
    )(q, k, v)


def _na_bias_table(rpb):
    depth = rpb.shape[0]
    qc = np.arange(GRID_W)[:, None]
    kc = np.arange(GRID_W)[None, :]
    c0 = np.clip(qc - NA_WIN_COLS // 2, 0, GRID_W - NA_WIN_COLS)
    mask = (kc >= c0) & (kc < c0 + NA_WIN_COLS)
    dc = kc - qc + NA_WIN_COLS - 1
    n_dr, n_dc = 2 * NA_WIN_ROWS - 1, 2 * NA_WIN_COLS - 1
    sel_c = (np.arange(n_dc)[:, None, None] == dc[None]) & mask[None]
    dr = np.arange(NA_WIN_ROWS)[:, None] + np.arange(NA_WIN_ROWS)[None, :]
    sel_r = np.arange(n_dr)[:, None, None] == dr[None]
    t = jnp.einsum("lhrc,rdk,cqn->lhdqkn", rpb, jnp.asarray(sel_r, F32), jnp.asarray(sel_c, F32),
                   precision=_HI)
    t = jnp.where(mask[:, None, :], t, NEG)
    return t.reshape(depth * NA_H, NA_WIN_ROWS, GRID_W, NA_BAND)


def _outproj_kernel(x_ref, ya_ref, of_ref, ob_ref, z_ref, yc_ref, g_ref, gn_ref, w_ref, o_ref):
    acc = _dot(ya_ref[0].astype(BF16), w_ref[0, 0:POOL_W, :])
    o = of_ref[0] + ob_ref[0]
    z = z_ref[0]
    for h in range(GDN_H):
        cols = slice(h * GDN_DH, (h + 1) * GDN_DH)
        oh = o[:, cols]
        on = oh * lax.rsqrt(jnp.mean(oh * oh, axis=-1, keepdims=True) + EPS) * gn_ref[0]
        yb = (on * _silu(z[:, cols])).astype(BF16)
        acc = acc + _dot(yb, w_ref[0, POOL_W + h * GDN_DH:POOL_W + (h + 1) * GDN_DH, :])
    acc = acc + _dot(yc_ref[0].astype(BF16), w_ref[0, POOL_W + GDN_W:, :])
    o_ref[0] = x_ref[0] + g_ref[0] * acc


def _outproj_call(l, xs, mod, mod_row, ya, o_f, o_b, gz, yc, gdn_norm, w_out, in_place):
    B, rows, _ = xs.shape
    tm = _tile(rows)

    def rowspec(n):
        return pl.BlockSpec((1, tm, n), lambda b, i: (b, i, 0))

    return pl.pallas_call(
        _outproj_kernel,
        grid=(B, rows // tm),
        in_specs=[rowspec(D_MODEL), rowspec(POOL_W), rowspec(GDN_W), rowspec(GDN_W), rowspec(GDN_W),
                  rowspec(NA_W), _mod_spec(2, mod_row),
                  pl.BlockSpec((1, 1, GDN_DH), lambda b, i: (l, 0, 0)),
                  pl.BlockSpec((1, D_MODEL, D_MODEL), lambda b, i: (l, 0, 0), **_SINGLE)],
        out_specs=rowspec(D_MODEL),
        out_shape=jax.ShapeDtypeStruct(xs.shape, F32),
        input_output_aliases={0: 0} if in_place else {},
        compiler_params=_params(("parallel", "parallel")),
        name="outproj",
    )(xs, ya, o_f, o_b, gz, yc, mod, gdn_norm, w_out)


def _ffn_kernel(cur_ref, prev_ref, next_ref, sh_ref, sc_ref, g_ref, nw_ref, wu_ref, cw_ref, wd_ref,
                o_ref, xbuf, hbuf, ua_buf, ub_buf, act_buf, *, tm, n_tiles):
    i = pl.program_id(1)
    _fill_halo_buf(xbuf, prev_ref, cur_ref, next_ref, i, n_tiles, tm)
    x = xbuf[...]
    r = lax.rsqrt(jnp.mean(x * x, axis=-1, keepdims=True) + EPS)
    h = (x * r * nw_ref[0]) * (1.0 + sc_ref[0]) + sh_ref[0]
    rows = lax.broadcasted_iota(jnp.int32, (tm + 2 * HALO, 1), 0)
    ok = jnp.logical_and(jnp.logical_or(rows >= HALO, i > 0),
                         jnp.logical_or(rows < HALO + tm, i < n_tiles - 1))
    hbuf[...] = jnp.where(ok, h, 0.0).astype(BF16)
    n_ff = D_FF // FF_TILE

    def up(j):
        hb = hbuf[...]
        ua_buf[j % 2] = _dot(hb, wu_ref[0, :, j * FF_TILE:(j + 1) * FF_TILE])
        ub_buf[j % 2] = _dot(hb, wu_ref[0, :, D_FF + j * FF_TILE:D_FF + (j + 1) * FF_TILE])

    up(0)
    for j in range(n_ff):
        if j + 1 < n_ff:
            up(j + 1)
        ca = slice(j * FF_TILE, (j + 1) * FF_TILE)
        cb = slice(D_FF + j * FF_TILE, D_FF + (j + 1) * FF_TILE)
        a = ua_buf[j % 2, pl.ds(HALO - 1, tm), :] * cw_ref[0, 0:1, ca]
        b = ub_buf[j % 2, pl.ds(HALO - 1, tm), :] * cw_ref[0, 0:1, cb]
        for k in range(1, FFN_CONV):
            off = HALO + k - FFN_CONV // 2
            a = a + ua_buf[j % 2, pl.ds(off, tm), :] * cw_ref[0, k:k + 1, ca]
            b = b + ub_buf[j % 2, pl.ds(off, tm), :] * cw_ref[0, k:k + 1, cb]
        act_buf[:, ca] = (_silu(a) * b).astype(BF16)
    o_ref[0] = cur_ref[0] + g_ref[0] * _dot(act_buf[...], wd_ref[0])


def _ffn_call(l, xs, mod, mod_row, norm_ffn, w_up, ffn_conv, w_down):
    B, rows, _ = xs.shape
    tm = _tile(rows)
    n = rows // tm
    prev, nxt = _halo_specs(tm, rows, D_MODEL)
    return pl.pallas_call(
        functools.partial(_ffn_kernel, tm=tm, n_tiles=n),
        grid=(B, n),
        in_specs=[pl.BlockSpec((1, tm, D_MODEL), lambda b, i: (b, i, 0)), prev, nxt,
                  _mod_spec(3, mod_row), _mod_spec(4, mod_row), _mod_spec(5, mod_row),
                  pl.BlockSpec((1, 1, D_MODEL), lambda b, i: (l, 0, 0)),
                  pl.BlockSpec((1, D_MODEL, 2 * D_FF), lambda b, i: (l, 0, 0), **_SINGLE),
                  pl.BlockSpec((1, FFN_CONV, 2 * D_FF), lambda b, i: (l, 0, 0)),
                  pl.BlockSpec((1, D_FF, D_MODEL), lambda b, i: (l, 0, 0), **_SINGLE)],
        out_specs=pl.BlockSpec((1, tm, D_MODEL), lambda b, i: (b, i, 0)),
        out_shape=jax.ShapeDtypeStruct(xs.shape, F32),
        scratch_shapes=[pltpu.VMEM((tm + 2 * HALO, D_MODEL), F32),
                        pltpu.VMEM((tm + 2 * HALO, D_MODEL), BF16),
                        pltpu.VMEM((2, tm + 2 * HALO, FF_TILE), F32),
                        pltpu.VMEM((2, tm + 2 * HALO, FF_TILE), F32),
                        pltpu.VMEM((tm, D_FF), BF16)],
        compiler_params=_params(("parallel", "parallel")),
        name="convffn",
    )(xs, xs, xs, mod, mod, mod, norm_ffn, w_up, ffn_conv, w_down)


def _rope_tables():
    half = GDN_DH // 2
    nf = half // 2
    t = jnp.arange(SEQ)
    freqs = ROPE_THETA ** (-jnp.arange(nf, dtype=F32) / nf)

    def part(pos):
        ang = pos.astype(F32)[:, None] * freqs
        cos, sin = jnp.cos(ang), jnp.sin(ang)
        return jnp.concatenate([cos, cos], axis=-1), jnp.concatenate([-sin, sin], axis=-1)

    c_r, s_r = part(t // GRID_W)
    c_c, s_c = part(t % GRID_W)
    return jnp.concatenate([c_r, c_c], axis=-1), jnp.concatenate([s_r, s_c], axis=-1)


def _pad_lanes(a, n=LANES):
    return jnp.pad(a, [(0, 0)] * (a.ndim - 1) + [(0, n - a.shape[-1])])


def kernel(x, c, ctx, c_ctx, w_ada, b_ada, norm_mix, w_in, pool_w, pool_scale, gdn_conv, gdn_a_log,
           gdn_dt_bias, gdn_norm, na_q_norm, na_k_norm, na_rpb, w_out, norm_ffn, w_up, ffn_conv, w_down):
    B = x.shape[0]
    depth = w_ada.shape[0]

    rows = -(-(B + 1) // SUBLANES) * SUBLANES
    cvec = jnp.concatenate([c, c_ctx[None, :], jnp.zeros((rows - B - 1, D_MODEL), F32)], axis=0)
    w_in_b = w_in.astype(BF16)
    wts = (w_in_b[:, :, :POOL_END], w_in_b[:, :, POOL_END:GQKV_END], w_in_b[:, :, GQKV_END:GZ_END],
           _pad_lanes(w_in_b[:, :, GZ_END:GAB_END]), w_in_b[:, :, GAB_END:])
    eye_g = jnp.eye(len(POOL_WINDOWS), dtype=F32)
    pool_bd = jnp.einsum("lgcd,gh->lgchd", pool_w, eye_g).reshape(depth, POOL_W, POOL_W).astype(BF16)
    alog = _pad_lanes(gdn_a_log.reshape(depth, 1, 2 * GDN_H))
    dtb = _pad_lanes(gdn_dt_bias.reshape(depth, 1, 2 * GDN_H))
    rope_tables = _rope_tables()
    qn_t = jnp.tile(na_q_norm, (1, NA_H)).reshape(depth, 1, NA_W)
    kn_t = jnp.tile(na_k_norm, (1, NA_H)).reshape(depth, 1, NA_W)
    w_out_b = w_out.astype(BF16)
    w_up_b = w_up.astype(BF16)
    w_down_b = w_down.astype(BF16)
    norm_mix3 = norm_mix.reshape(depth, 1, D_MODEL)
    norm_ffn3 = norm_ffn.reshape(depth, 1, D_MODEL)
    pool_scale3 = pool_scale.reshape(depth, 1, POOL_W)
    gdn_norm3 = gdn_norm.reshape(depth, 1, GDN_DH)
    s_zero = jnp.zeros((B, N_CHAIN, GDN_DH, GDN_DH), F32)
    na_bias = _na_bias_table(na_rpb)

    mods = _ada_call(cvec, w_ada, b_ada).reshape(depth, rows, 1, 6 * D_MODEL)

    for l in range(depth):
        mod = mods[l]
        with_ctx_out = l < depth - 1
        pv, gqkv, gz, gab, naq, nak, nav = _inproj_call(l, x, mod, None, norm_mix3, wts, qn_t, kn_t)
        cpv, cgqkv, cgz, cgab, cnaq, cnak, cnav = _inproj_call(l, ctx, mod, B, norm_mix3, wts, qn_t, kn_t)
        cprep = _gdn_prep_call(l, cgqkv, cgab, gdn_conv, alog, dtb, None)
        co_f, co_b, s_ctx = _gdn_scan_call(cprep, s_zero)
        prep = _gdn_prep_call(l, gqkv, gab, gdn_conv, alog, dtb, rope_tables)
        o_f, o_b, _ = _gdn_scan_call(prep, s_ctx)
        ya = _pool_call(l, pv, pool_bd, pool_scale3)
        yc = _na_call(l, naq, nak, nav, cnak, cnav, na_bias)
        x = _outproj_call(l, x, mod, None, ya, o_f, o_b, gz, yc, gdn_norm3, w_out_b, l > 0)
        x = _ffn_call(l, x, mod, None, norm_ffn3, w_up_b, ffn_conv, w_down_b)
        if with_ctx_out:
            cya = _pool_call(l, cpv, pool_bd, pool_scale3)
            cyc = _ctx_attn_call(cnaq, cnak, cnav)
            ctx = _outproj_call(l, ctx, mod, B, cya, co_f, co_b, cgz, cyc, gdn_norm3, w_out_b, l > 0)
            ctx = _ffn_call(l, ctx, mod, B, norm_ffn3, w_up_b, ffn_conv, w_down_b)
    return x
```

```python
import functools

import numpy as np
import jax
import jax.numpy as jnp
from jax import lax
from jax.experimental import pallas as pl
from jax.experimental.pallas import tpu as pltpu

F32 = jnp.float32
BF16 = jnp.bfloat16

D_MODEL = 1024
SEQ = 4096
CTX = 256
GRID_W = 64
EPS = 1e-6

POOL_W = 256
POOL_WINDOWS = (2, 4, 8, 16)
POOL_GD = 64
GDN_W = 512
GDN_H = 4
GDN_DH = 128
CHUNK = 64
SHORT_CONV = 5
ROPE_THETA = 10000.0
NA_W = 256
NA_H = 4
NA_DH = 64
NA_WIN_ROWS = 8
NA_WIN_COLS = 16
D_FF = 2816
FFN_CONV = 3

POOL_END = POOL_W
GQKV_END = POOL_END + 3 * GDN_W
GZ_END = GQKV_END + GDN_W
GAB_END = GZ_END + 4 * GDN_H

LANES = 128
SUBLANES = 8
HALO = SUBLANES
TM_MM = 512
TM_GDN = 256
FF_TILE = 256
NEG = -1e30
VMEM_LIMIT = 56 * 1024 * 1024

_HI = lax.Precision.HIGHEST
_SINGLE = dict(pipeline_mode=pl.Buffered(1))


def _dot(a, b):
    return jnp.dot(a, b, preferred_element_type=F32)


def _silu(x):
    return x * jax.nn.sigmoid(x)


def _params(sem, vmem=VMEM_LIMIT):
    return pltpu.CompilerParams(dimension_semantics=sem, vmem_limit_bytes=vmem)


def _tile(rows):
    return min(rows, TM_MM)


def _mod_spec(chunk, row):
    return pl.BlockSpec((1, 1, D_MODEL), lambda b, i: (b if row is None else row, 0, chunk))


def _halo_specs(tm, rows, width, col=None):
    nb = tm // HALO
    last = rows // HALO - 1
    c = 0 if col is None else col
    prev = pl.BlockSpec((1, HALO, width), lambda b, i: (b, jnp.maximum(i * nb - 1, 0), c))
    nxt = pl.BlockSpec((1, HALO, width), lambda b, i: (b, jnp.minimum((i + 1) * nb, last), c))
    return prev, nxt


def _fill_halo_buf(buf, prev_ref, cur_ref, next_ref, i, n_tiles, tm):
    buf[0:HALO, :] = jnp.where(i > 0, prev_ref[0], 0.0)
    buf[HALO:HALO + tm, :] = cur_ref[0]
    buf[HALO + tm:2 * HALO + tm, :] = jnp.where(i < n_tiles - 1, next_ref[0], 0.0)


def _ada_kernel(c_ref, w_ref, b_ref, o_ref):
    sc = _silu(c_ref[...]).astype(BF16)
    o_ref[0] = _dot(sc, w_ref[0].astype(BF16)) + b_ref[0]


def _ada_call(cvec, w_ada, b_ada):
    depth = w_ada.shape[0]
    rows = cvec.shape[0]
    tn = 1536
    return pl.pallas_call(
        _ada_kernel,
        grid=(depth, 6 * D_MODEL // tn),
        in_specs=[
            pl.BlockSpec((rows, D_MODEL), lambda l, j: (0, 0)),
            pl.BlockSpec((1, D_MODEL, tn), lambda l, j: (l, 0, j)),
            pl.BlockSpec((1, 1, tn), lambda l, j: (l, 0, j)),
        ],
        out_specs=pl.BlockSpec((1, rows, tn), lambda l, j: (l, 0, j)),
        out_shape=jax.ShapeDtypeStruct((depth, rows, 6 * D_MODEL), F32),
        compiler_params=_params(("parallel", "parallel")),
        name="adaln",
    )(cvec, w_ada, b_ada.reshape(depth, 1, 6 * D_MODEL))


def _inproj_kernel(*refs, tm, n_tiles, rope):
    if rope:
        (x_ref, xp_ref, xn_ref, sh_ref, sc_ref, nw_ref, wp_ref, wq_ref, wz_ref, wab_ref, wna_ref,
         qn_ref, kn_ref, cw_ref, cos_ref, sin_ref,
         pool_o, gqkv_o, gz_o, gab_o, naq_o, nak_o, nav_o, xbuf) = refs
    else:
        (x_ref, xp_ref, xn_ref, sh_ref, sc_ref, nw_ref, wp_ref, wq_ref, wz_ref, wab_ref, wna_ref,
         qn_ref, kn_ref, cw_ref,
         pool_o, gqkv_o, gz_o, gab_o, naq_o, nak_o, nav_o, xbuf) = refs
    i = pl.program_id(1)
    _fill_halo_buf(xbuf, xp_ref, x_ref, xn_ref, i, n_tiles, tm)
    x = xbuf[...]
    r = lax.rsqrt(jnp.mean(x * x, axis=-1, keepdims=True) + EPS)
    h = (x * r * nw_ref[0]) * (1.0 + sc_ref[0]) + sh_ref[0]
    rows = lax.broadcasted_iota(jnp.int32, (tm + 2 * HALO, 1), 0)
    ok = jnp.logical_and(jnp.logical_or(rows >= HALO, i > 0),
                         jnp.logical_or(rows < HALO + tm, i < n_tiles - 1))
    hb_full = jnp.where(ok, h, 0.0).astype(BF16)
    hb = hb_full[HALO:HALO + tm]

    p_qkv = _dot(hb_full, wq_ref[0])
    nb = tm + 2 * HALO
    acc = None
    for k in range(SHORT_CONV):
        shift = (SHORT_CONV // 2 - k) % nb
        xs = p_qkv if shift == 0 else pltpu.roll(p_qkv, shift, axis=0)
        term = xs[HALO:HALO + tm] * cw_ref[0, k:k + 1, :]
        acc = term if acc is None else acc + term
    qkv = _silu(acc)
    lane = lax.broadcasted_iota(jnp.int32, (tm, GDN_DH), 1)
    first = (lane % (GDN_DH // 2)) < (GDN_DH // 4)

    def norm_rope(xv):
        xn = xv * lax.rsqrt(jnp.sum(xv * xv, axis=-1, keepdims=True) + EPS)
        if not rope:
            return xn
        partner = jnp.where(first, pltpu.roll(xn, GDN_DH - GDN_DH // 4, axis=1),
                            pltpu.roll(xn, GDN_DH // 4, axis=1))
        return xn * cos_ref[...] + partner * sin_ref[...]

    for hd in range(GDN_H):
        cq = slice(hd * GDN_DH, (hd + 1) * GDN_DH)
        ck = slice(GDN_W + hd * GDN_DH, GDN_W + (hd + 1) * GDN_DH)
        gqkv_o[0, :, cq] = norm_rope(qkv[:, cq]) * (GDN_DH ** -0.5)
        gqkv_o[0, :, ck] = norm_rope(qkv[:, ck])
    gqkv_o[0, :, 2 * GDN_W:] = qkv[:, 2 * GDN_W:]

    pool_o[0] = _dot(hb, wp_ref[0])
    gz_o[0] = _dot(hb, wz_ref[0])
    gab_o[0] = _dot(hb, wab_ref[0])
    na = _dot(hb, wna_ref[0])
    li = lax.broadcasted_iota(jnp.int32, (NA_W, NA_W), 0) // NA_DH
    lj = lax.broadcasted_iota(jnp.int32, (NA_W, NA_W), 1) // NA_DH
    seg = (li == lj).astype(BF16)
    q = na[:, 0:NA_W]
    k = na[:, NA_W:2 * NA_W]

    def seg_mean(sq):
        hi = sq.astype(BF16)
        lo = (sq - hi.astype(F32)).astype(BF16)
        return (_dot(hi, seg) + _dot(lo, seg)) * (1.0 / NA_DH)

    qs = seg_mean(q * q)
    ks = seg_mean(k * k)
    qn = q * lax.rsqrt(qs + EPS) * qn_ref[0] * (NA_DH ** -0.5)
    kn = k * lax.rsqrt(ks + EPS) * kn_ref[0]
    v = na[:, 2 * NA_W:3 * NA_W]

    def with_swizzle(t):
        return jnp.concatenate([t, pltpu.roll(t, NA_DH, axis=1)], axis=1).astype(BF16)

    naq_o[0] = with_swizzle(qn)
    nak_o[0] = with_swizzle(kn)
    nav_o[0] = with_swizzle(v)


def _inproj_call(l, xs, mod, mod_row, norm_mix, wts, qn_t, kn_t, conv_w, rope_tables):
    B, rows, _ = xs.shape
    tm = _tile(rows)
    n = rows // tm
    rope = rope_tables is not None
    prev, nxt = _halo_specs(tm, rows, D_MODEL)

    def wspec(w):
        return pl.BlockSpec((1,) + w.shape[1:], lambda b, i: (l, 0, 0), **_SINGLE)

    def rowspec(width):
        return pl.BlockSpec((1, tm, width), lambda b, i: (b, i, 0))

    def vec(width, sub=1):
        return pl.BlockSpec((1, sub, width), lambda b, i: (l, 0, 0))

    in_specs = ([rowspec(D_MODEL), prev, nxt, _mod_spec(0, mod_row), _mod_spec(1, mod_row), vec(D_MODEL)]
                + [wspec(w) for w in wts] + [vec(NA_W), vec(NA_W), vec(3 * GDN_W, SHORT_CONV)])
    args = [xs, xs, xs, mod, mod, norm_mix, *wts, qn_t, kn_t, conv_w]
    if rope:
        in_specs += [pl.BlockSpec((tm, GDN_DH), lambda b, i: (i, 0))] * 2
        args += list(rope_tables)
    outs = [(POOL_W, F32), (3 * GDN_W, F32), (GDN_W, F32), (LANES, F32),
            (2 * NA_W, BF16), (2 * NA_W, BF16), (2 * NA_W, BF16)]
    return pl.pallas_call(
        functools.partial(_inproj_kernel, tm=tm, n_tiles=n, rope=rope),
        grid=(B, n),
        in_specs=in_specs,
        out_specs=[rowspec(width) for width, _ in outs],
        out_shape=[jax.ShapeDtypeStruct((B, rows, width), dt) for width, dt in outs],
        scratch_shapes=[pltpu.VMEM((tm + 2 * HALO, D_MODEL), F32)],
        compiler_params=_params(("parallel", "parallel")),
        name="inproj",
    )(*args)


def _pool_kernel(cur_ref, prev_ref, next_ref, w_ref, s_ref, o_ref, buf, *, tm, rows):
    i = pl.program_id(1)
    _fill_halo_buf(buf, prev_ref, cur_ref, next_ref, i, rows // tm, tm)
    v = cur_ref[0]
    t = lax.broadcasted_iota(jnp.int32, (tm, POOL_W), 0) + i * tm
    lane = lax.broadcasted_iota(jnp.int32, (tm, POOL_W), 1)
    half = jnp.left_shift(1, lane // POOL_GD)
    cnt = (jnp.minimum(t + half, rows) - jnp.maximum(t - half, 0)).astype(F32)

    nb = tm + 2 * HALO
    xb = buf[...]

    def shifted(a, k):
        return pltpu.roll(a, k % nb, axis=0)

    a2 = xb + shifted(xb, 1)
    a4 = shifted(a2, 1) + shifted(a2, -1)
    a8 = shifted(a4, 2) + shifted(a4, -2)
    a16 = shifted(a8, 4) + shifted(a8, -4)
    grp = lane // POOL_GD
    pooled = jnp.where(grp == 0, a2[HALO:HALO + tm],
                       jnp.where(grp == 1, a4[HALO:HALO + tm],
                                 jnp.where(grp == 2, a8[HALO:HALO + tm], a16[HALO:HALO + tm])))
    pooled = pooled / cnt - v
    o_ref[0] = _dot(pooled.astype(BF16), w_ref[0]) * s_ref[0]


def _pool_call(l, pv, w_bd, scale):
    B, rows, _ = pv.shape
    tm = _tile(rows)
    prev, nxt = _halo_specs(tm, rows, POOL_W)
    return pl.pallas_call(
        functools.partial(_pool_kernel, tm=tm, rows=rows),
        grid=(B, rows // tm),
        in_specs=[
            pl.BlockSpec((1, tm, POOL_W), lambda b, i: (b, i, 0)), prev, nxt,
            pl.BlockSpec((1, POOL_W, POOL_W), lambda b, i: (l, 0, 0)),
            pl.BlockSpec((1, 1, POOL_W), lambda b, i: (l, 0, 0)),
        ],
        out_specs=pl.BlockSpec((1, tm, POOL_W), lambda b, i: (b, i, 0)),
        out_shape=jax.ShapeDtypeStruct((B, rows, POOL_W), F32),
        scratch_shapes=[pltpu.VMEM((tm + 2 * HALO, POOL_W), F32)],
        compiler_params=_params(("parallel", "parallel")),
        name="pool",
    )(pv, pv, pv, w_bd, scale)


CPT = TM_GDN // CHUNK
N_CHAIN = 2 * GDN_H
SCAN_NB = 2


def _bdot(a, b):
    return lax.dot_general(a.astype(BF16), b.astype(BF16), (((2,), (1,)), ((0,), (0,))),
                           preferred_element_type=F32)


def _bdot_nt(a, b):
    return lax.dot_general(a.astype(BF16), b.astype(BF16), (((2,), (2,)), ((0,), (0,))),
                           preferred_element_type=F32)


def _split3(x):
    hi = x.astype(BF16)
    r1 = x - hi.astype(F32)
    mid = r1.astype(BF16)
    lo = (r1 - mid.astype(F32)).astype(BF16)
    return hi, mid, lo


def _gdn_prep_kernel(qkv_ref, gab_ref, alog_ref, dtb_ref,
                     uf_o, ub_o, wqf_o, wqb_o, ikf_o, ikb_o, gl_o, gt_buf):
    tm = TM_GDN
    qkv = qkv_ref[0]

    gab = gab_ref[0]
    g_all = -jnp.exp(alog_ref[0]) * jax.nn.softplus(gab + dtb_ref[0])
    beta_all = jax.nn.sigmoid(gab)
    ri = lax.broadcasted_iota(jnp.int32, (tm, tm), 0)
    rj = lax.broadcasted_iota(jnp.int32, (tm, tm), 1)
    same = (ri // CHUNK) == (rj // CHUNK)
    low = jnp.logical_and(same, rj <= ri).astype(BF16)
    upp = jnp.logical_and(same, rj >= ri).astype(BF16)
    g_hi, g_mid, g_lo = _split3(g_all)
    pre = _dot(low, g_hi) + _dot(low, g_mid) + _dot(low, g_lo)
    suf = _dot(upp, g_hi) + _dot(upp, g_mid) + _dot(upp, g_lo)
    glane = lax.broadcasted_iota(jnp.int32, (tm, LANES), 1)
    gc_all = jnp.where(glane < GDN_H, pre, suf)
    gt_buf[...] = gc_all.T

    pi = lax.broadcasted_iota(jnp.int32, (CHUNK, 2 * CHUNK), 0)
    pj = lax.broadcasted_iota(jnp.int32, (CHUNK, 2 * CHUNK), 1)
    is_fwd = pj < CHUNK
    pjj = jnp.where(is_fwd, pj, pj - CHUNK)
    sgn = jnp.where(is_fwd, 1, -1)
    incl = (pi - pjj) * sgn >= 0
    strict = (pi - pjj) * sgn > 0

    fwd_half = jnp.where(is_fwd, 1.0, 0.0).astype(BF16)
    bwd_half = jnp.where(is_fwd, 0.0, 1.0).astype(BF16)

    def block_diag(x):
        xb16 = x.astype(BF16)
        return jnp.concatenate([xb16 * fwd_half, xb16 * bwd_half], axis=1)

    lhs_l, k2_l, decay_l, beta_l, rhs_l, qd_l, kd_l, gl_l = [], [], [], [], [], [], [], []
    for h in range(GDN_H):
        q3 = qkv[:, h * GDN_DH:(h + 1) * GDN_DH].reshape(CPT, CHUNK, GDN_DH)
        k3 = qkv[:, GDN_W + h * GDN_DH:GDN_W + (h + 1) * GDN_DH].reshape(CPT, CHUNK, GDN_DH)
        v3 = qkv[:, 2 * GDN_W + h * GDN_DH:2 * GDN_W + (h + 1) * GDN_DH].reshape(CPT, CHUNK, GDN_DH)
        kb16 = k3.astype(BF16)
        lhs_l.append(jnp.concatenate([kb16, q3.astype(BF16)], axis=1))
        k2_l.append(jnp.concatenate([kb16, kb16], axis=1))
        gcols, betas, grows, rhs_d = [], [], [], []
        for d in range(2):
            gi = d * GDN_H + h
            bi = 2 * GDN_H + gi
            gcol = jnp.broadcast_to(gc_all[:, gi:gi + 1], (tm, GDN_DH)).reshape(CPT, CHUNK, GDN_DH)
            beta = jnp.broadcast_to(beta_all[:, bi:bi + 1], (tm, GDN_DH)).reshape(CPT, CHUNK, GDN_DH)
            grow_full = gt_buf[gi:gi + 1, :]
            grows.append([grow_full[:, c * CHUNK:(c + 1) * CHUNK] for c in range(CPT)])
            e = jnp.exp(gcol)
            rhs_d.append(jnp.concatenate([v3 * beta, k3 * (beta * e)], axis=2))
            g_last = gcol[:, 0:1, :] if d else gcol[:, CHUNK - 1:CHUNK, :]
            qd_l.append(q3 * e)
            kd_l.append(k3 * jnp.exp(g_last - gcol))
            gl_l.append(jnp.exp(g_last))
            gcols.append(gcol)
            betas.append(beta)
        grow = jnp.stack([jnp.concatenate([grows[0][c], grows[1][c]], axis=1) for c in range(CPT)],
                         axis=0)
        diff = jnp.where(is_fwd, gcols[0], gcols[1]) - grow
        decay_l.append(jnp.where(incl, jnp.exp(jnp.where(incl, diff, 0.0)), 0.0))
        beta_l.append(jnp.where(is_fwd, betas[0], betas[1]))
        rhs_l.append(jnp.concatenate(rhs_d, axis=1))

    decay = jnp.concatenate(decay_l, axis=0)
    beta_p = jnp.concatenate(beta_l, axis=0)
    m1 = _bdot_nt(jnp.concatenate(lhs_l, axis=0), jnp.concatenate(k2_l, axis=0))
    a_mat = jnp.where(strict, m1[:, :CHUNK] * (decay * beta_p), 0.0)
    intra = m1[:, CHUNK:] * decay
    p = -a_mat
    t_off = p
    for _ in range(5):
        p = _bdot(p, block_diag(p))
        t_off = t_off + p + _bdot(t_off, block_diag(p))
    rhs = jnp.concatenate(rhs_l, axis=0)
    uw = rhs + _bdot(block_diag(t_off), rhs)

    outs = ((uf_o, wqf_o, ikf_o), (ub_o, wqb_o, ikb_o))
    for d in range(2):
        u_o, wq_o, ik_o = outs[d]
        prow = slice(d * CHUNK, (d + 1) * CHUNK)
        for h in range(GDN_H):
            n0 = h * CPT
            cols = slice(h * GDN_DH, (h + 1) * GDN_DH)
            u_o[0, :, cols] = uw[n0:n0 + CPT, prow, :GDN_DH].reshape(tm, GDN_DH)
            wq_o[0, h, :, 0:CHUNK, :] = uw[n0:n0 + CPT, prow, GDN_DH:].astype(BF16)
            wq_o[0, h, :, CHUNK:, :] = qd_l[2 * h + d].astype(BF16)
            ik_o[0, h, :, 0:CHUNK, :] = intra[n0:n0 + CPT, :, prow].astype(BF16)
            kd3 = kd_l[2 * h + d]
            for c in range(CPT):
                ik_o[0, h, c, CHUNK:, :] = kd3[c].T.astype(BF16)
            gl_o[0, h, :, d:d + 1, :] = gl_l[2 * h + d]


def _gdn_prep_call(l, qkv, gab, alog, dtb):
    B, rows, _ = qkv.shape
    tm = TM_GDN
    n = rows // tm

    def vec(width):
        return pl.BlockSpec((1, 1, width), lambda b, i: (l, 0, 0))

    in_specs = [pl.BlockSpec((1, tm, 3 * GDN_W), lambda b, i: (b, i, 0)),
                pl.BlockSpec((1, tm, LANES), lambda b, i: (b, i, 0)), vec(LANES), vec(LANES)]
    args = [qkv, gab, alog, dtb]
    nc = rows // CHUNK
    u_spec = pl.BlockSpec((1, tm, GDN_W), lambda b, i: (b, i, 0))
    wq_spec = pl.BlockSpec((1, GDN_H, CPT, 2 * CHUNK, GDN_DH), lambda b, i: (b, 0, i, 0, 0))
    ik_spec = pl.BlockSpec((1, GDN_H, CPT, 3 * CHUNK, CHUNK), lambda b, i: (b, 0, i, 0, 0))
    gl_spec = pl.BlockSpec((1, GDN_H, CPT, 2, LANES), lambda b, i: (b, 0, i, 0, 0))
    u_sh = jax.ShapeDtypeStruct((B, rows, GDN_W), F32)
    wq_sh = jax.ShapeDtypeStruct((B, GDN_H, nc, 2 * CHUNK, GDN_DH), BF16)
    ik_sh = jax.ShapeDtypeStruct((B, GDN_H, nc, 3 * CHUNK, CHUNK), BF16)
    gl_sh = jax.ShapeDtypeStruct((B, GDN_H, nc, 2, LANES), F32)
    return pl.pallas_call(
        _gdn_prep_kernel,
        grid=(B, n),
        in_specs=in_specs,
        out_specs=[u_spec, u_spec, wq_spec, wq_spec, ik_spec, ik_spec, gl_spec],
        out_shape=[u_sh, u_sh, wq_sh, wq_sh, ik_sh, ik_sh, gl_sh],
        scratch_shapes=[pltpu.VMEM((LANES, tm), F32)],
        compiler_params=_params(("parallel", "parallel")),
        name="gdn_prep",
    )(*args)


def _gdn_scan_kernel(uf_ref, ub_ref, wqf_ref, wqb_ref, ikf_ref, ikb_ref, glf_ref, glb_ref, s0_ref,
                     of_o, ob_o, s_o, s_ref, *, n_tiles, nb):
    i = pl.program_id(1)

    @pl.when(i == 0)
    def _():
        s_ref[...] = s0_ref[...]

    refs = ((uf_ref, wqf_ref, ikf_ref, glf_ref, of_o), (ub_ref, wqb_ref, ikb_ref, glb_ref, ob_o))
    chains = [(e, d, h) for e in range(nb) for d in range(2) for h in range(GDN_H)]
    states = [s_ref[e, d * GDN_H + h] for e, d, h in chains]
    for step in range(CPT):
        ws_qs = []
        for n, (e, d, h) in enumerate(chains):
            c = CPT - 1 - step if d else step
            ws_qs.append(_dot(refs[d][1][e, h, c], states[n].astype(BF16)))
        vb = []
        for n, (e, d, h) in enumerate(chains):
            c = CPT - 1 - step if d else step
            u = refs[d][0][e, c * CHUNK:(c + 1) * CHUNK, h * GDN_DH:(h + 1) * GDN_DH]
            vb.append((u - ws_qs[n][:CHUNK]).astype(BF16))
        r = []
        for n, (e, d, h) in enumerate(chains):
            c = CPT - 1 - step if d else step
            r.append(_dot(refs[d][2][e, h, c], vb[n]))
        for n, (e, d, h) in enumerate(chains):
            c = CPT - 1 - step if d else step
            refs[d][4][e, c * CHUNK:(c + 1) * CHUNK, h * GDN_DH:(h + 1) * GDN_DH] = (
                ws_qs[n][CHUNK:] + r[n][:CHUNK])
            states[n] = states[n] * refs[d][3][e, h, c, d:d + 1, :] + r[n][CHUNK:]
    for n, (e, d, h) in enumerate(chains):
        s_ref[e, d * GDN_H + h] = states[n]

    @pl.when(i == n_tiles - 1)
    def _():
        s_o[...] = s_ref[...]


def _gdn_scan_call(prep, s0):
    uf, ub, wqf, wqb, ikf, ikb, gl = prep
    B, rows, _ = uf.shape
    tm = TM_GDN
    n = rows // tm
    nb = SCAN_NB if B % SCAN_NB == 0 else 1

    def ident(i):
        return i

    def rev(i):
        return n - 1 - i

    def rowspec(order):
        return pl.BlockSpec((nb, tm, GDN_W), lambda b, i: (b, order(i), 0))

    def tilespec(arr, order):
        return pl.BlockSpec((nb, GDN_H, CPT) + arr.shape[3:], lambda b, i: (b, 0, order(i), 0, 0))

    s_spec = pl.BlockSpec((nb, N_CHAIN, GDN_DH, GDN_DH), lambda b, i: (b, 0, 0, 0))
    o_shape = jax.ShapeDtypeStruct((B, rows, GDN_W), F32)
    return pl.pallas_call(
        functools.partial(_gdn_scan_kernel, n_tiles=n, nb=nb),
        grid=(B // nb, n),
        in_specs=[rowspec(ident), rowspec(rev), tilespec(wqf, ident), tilespec(wqb, rev),
                  tilespec(ikf, ident), tilespec(ikb, rev), tilespec(gl, ident), tilespec(gl, rev),
                  s_spec],
        out_specs=[rowspec(ident), rowspec(rev), s_spec],
        out_shape=[o_shape, o_shape, jax.ShapeDtypeStruct(s0.shape, F32)],
        scratch_shapes=[pltpu.VMEM((nb, N_CHAIN, GDN_DH, GDN_DH), F32)],
        compiler_params=_params(("parallel", "arbitrary")),
        name="gdn_scan",
    )(uf, ub, wqf, wqb, ikf, ikb, gl, gl, s0)


NA_ROWS = SEQ // GRID_W
NA_BAND = NA_WIN_ROWS * GRID_W
NA_RPS = 8
_NT = (((1,), (1,)), ((), ()))


def _na_head_cols(h):
    off = (h // 2) * LANES if h % 2 == 0 else NA_W + ((h // 2 + 1) % (NA_H // 2)) * LANES
    return slice(off, off + NA_DH)


def _lane_fold(parts, op):
    tiles = [p[:, t * LANES:(t + 1) * LANES] for p in parts for t in range(p.shape[1] // LANES)]
    acc = tiles[0]
    for t in tiles[1:]:
        acc = op(acc, t)
    return acc


def _na_kernel(q_ref, k_ref, v_ref, kc_ref, vc_ref, bias_ref, o_ref):
    step = pl.program_id(1)
    q = q_ref[0]
    kc = kc_ref[0]
    vc = vc_ref[0]
    heads = [_na_head_cols(h) for h in range(NA_H)]
    qrows = [slice(j * GRID_W, (j + 1) * GRID_W) for j in range(NA_RPS)]
    kb, vb, d0 = [], [], []
    for j in range(NA_RPS):
        r = step * NA_RPS + j
        r0 = jnp.clip(r - NA_WIN_ROWS // 2, 0, NA_ROWS - NA_WIN_ROWS)
        d0.append(r0 - r + NA_WIN_ROWS - 1)
        start = pl.multiple_of(r0 * GRID_W, GRID_W)
        kb.append(k_ref[0, pl.ds(start, NA_BAND), :])
        vb.append(v_ref[0, pl.ds(start, NA_BAND), :])

    def scores(j):
        return [lax.dot_general(q[qrows[j], c], kb[j][:, c], _NT, preferred_element_type=F32)
                for c in heads]

    s_ctx = [lax.dot_general(q[:, c], kc[:, c], _NT, preferred_element_type=F32) for c in heads]
    p_ctx = [[None] * NA_RPS for _ in range(NA_H)]
    den = [[None] * NA_H for _ in range(NA_RPS)]
    o_loc = [None] * NA_RPS
    s_next = scores(0)
    for j in range(NA_RPS):
        s_loc = s_next
        if j + 1 < NA_RPS:
            s_next = scores(j + 1)
        p_loc = []
        for h in range(NA_H):
            sl = s_loc[h] + bias_ref[h, d0[j]]
            sc = s_ctx[h][qrows[j]]
            m = jnp.max(_lane_fold([sl, sc], jnp.maximum), axis=-1, keepdims=True)
            pl_ = jnp.exp(sl - m)
            pc_ = jnp.exp(sc - m)
            den[j][h] = jnp.sum(_lane_fold([pl_, pc_], jnp.add), axis=-1, keepdims=True)
            p_loc.append(pl_.astype(BF16))
            p_ctx[h][j] = pc_.astype(BF16)
        o_loc[j] = [_dot(p_loc[h], vb[j][:, c]) for h, c in enumerate(heads)]
    o_ctx = [_dot(jnp.concatenate(p_ctx[h], axis=0), vc[:, c]) for h, c in enumerate(heads)]
    for j in range(NA_RPS):
        o_ref[0, qrows[j], :] = jnp.concatenate(
            [(o_loc[j][h] + o_ctx[h][qrows[j]]) / den[j][h] for h in range(NA_H)], axis=-1)


def _na_call(l, q, k, v, kc, vc, bias):
    B = q.shape[0]
    full = pl.BlockSpec((1, SEQ, 2 * NA_W), lambda b, r: (b, 0, 0))
    cfull = pl.BlockSpec((1, CTX, 2 * NA_W), lambda b, r: (b, 0, 0))
    blk = pl.BlockSpec((1, NA_RPS * GRID_W, 2 * NA_W), lambda b, r: (b, r, 0))
    oblk = pl.BlockSpec((1, NA_RPS * GRID_W, NA_W), lambda b, r: (b, r, 0))
    return pl.pallas_call(
        _na_kernel,
        grid=(B, NA_ROWS // NA_RPS),
        in_specs=[blk, full, full, cfull, cfull,
                  pl.BlockSpec((NA_H,) + bias.shape[1:], lambda b, r: (l, 0, 0, 0))],
        out_specs=oblk,
        out_shape=jax.ShapeDtypeStruct((B, SEQ, NA_W), F32),
        compiler_params=_params(("parallel", "arbitrary")),
        name="natten",
    )(q, k, v, kc, vc, bias)


def _ctx_attn_kernel(q_ref, k_ref, v_ref, o_ref):
    q, k, v = q_ref[0], k_ref[0], v_ref[0]
    outs = []
    for h in range(NA_H):
        cols = _na_head_cols(h)
        s = lax.dot_general(q[:, cols], k[:, cols], _NT, preferred_element_type=F32)
        p = jnp.exp(s - jnp.max(s, axis=-1, keepdims=True))
        den = jnp.sum(p, axis=-1, keepdims=True)
        outs.append(_dot(p.astype(BF16), v[:, cols]) / den)
    o_ref[0] = jnp.concatenate(outs, axis=-1)


def _ctx_attn_call(q, k, v):
    B = q.shape[0]
    spec = pl.BlockSpec((1, CTX, 2 * NA_W), lambda b: (b, 0, 0))
    ospec = pl.BlockSpec((1, CTX, NA_W), lambda b: (b, 0, 0))
    return pl.pallas_call(
        _ctx_attn_kernel,
        grid=(B,),
        in_specs=[spec, spec, spec],
        out_specs=ospec,
        out_shape=jax.ShapeDtypeStruct((B, CTX, NA_W), F32),
        compiler_params=_params(("parallel",)),
        name="ctx_attn",
    )(q, k, v)


def _na_bias_table(rpb):
    depth = rpb.shape[0]
    qc = np.arange(GRID_W)[:, None]
    kc = np.arange(GRID_W)[None, :]
    c0 = np.clip(qc - NA_WIN_COLS // 2, 0, GRID_W - NA_WIN_COLS)
    mask = (kc >= c0) & (kc < c0 + NA_WIN_COLS)
    dc = kc - qc + NA_WIN_COLS - 1
    n_dr, n_dc = 2 * NA_WIN_ROWS - 1, 2 * NA_WIN_COLS - 1
    sel_c = (np.arange(n_dc)[:, None, None] == dc[None]) & mask[None]
    dr = np.arange(NA_WIN_ROWS)[:, None] + np.arange(NA_WIN_ROWS)[None, :]
    sel_r = np.arange(n_dr)[:, None, None] == dr[None]
    t = jnp.einsum("lhrc,rdk,cqn->lhdqkn", rpb, jnp.asarray(sel_r, F32), jnp.asarray(sel_c, F32),
                   precision=_HI)
    t = jnp.where(mask[:, None, :], t, NEG)
    return t.reshape(depth * NA_H, NA_WIN_ROWS, GRID_W, NA_BAND)


def _outproj_kernel(x_ref, ya_ref, of_ref, ob_ref, z_ref, yc_ref, g_ref, gn_ref, w_ref, o_ref):
    acc = _dot(ya_ref[0].astype(BF16), w_ref[0, 0:POOL_W, :])
    o = of_ref[0] + ob_ref[0]
    z = z_ref[0]
    for h in range(GDN_H):
        cols = slice(h * GDN_DH, (h + 1) * GDN_DH)
        oh = o[:, cols]
        on = oh * lax.rsqrt(jnp.mean(oh * oh, axis=-1, keepdims=True) + EPS) * gn_ref[0]
        yb = (on * _silu(z[:, cols])).astype(BF16)
        acc = acc + _dot(yb, w_ref[0, POOL_W + h * GDN_DH:POOL_W + (h + 1) * GDN_DH, :])
    acc = acc + _dot(yc_ref[0].astype(BF16), w_ref[0, POOL_W + GDN_W:, :])
    o_ref[0] = x_ref[0] + g_ref[0] * acc


def _outproj_call(l, xs, mod, mod_row, ya, o_f, o_b, gz, yc, gdn_norm, w_out, in_place):
    B, rows, _ = xs.shape
    tm = _tile(rows)

    def rowspec(n):
        return pl.BlockSpec((1, tm, n), lambda b, i: (b, i, 0))

    return pl.pallas_call(
        _outproj_kernel,
        grid=(B, rows // tm),
        in_specs=[rowspec(D_MODEL), rowspec(POOL_W), rowspec(GDN_W), rowspec(GDN_W), rowspec(GDN_W),
                  rowspec(NA_W), _mod_spec(2, mod_row),
                  pl.BlockSpec((1, 1, GDN_DH), lambda b, i: (l, 0, 0)),
                  pl.BlockSpec((1, D_MODEL, D_MODEL), lambda b, i: (l, 0, 0), **_SINGLE)],
        out_specs=rowspec(D_MODEL),
        out_shape=jax.ShapeDtypeStruct(xs.shape, F32),
        input_output_aliases={0: 0} if in_place else {},
        compiler_params=_params(("parallel", "parallel")),
        name="outproj",
    )(xs, ya, o_f, o_b, gz, yc, mod, gdn_norm, w_out)


def _ffn_kernel(cur_ref, prev_ref, next_ref, sh_ref, sc_ref, g_ref, nw_ref, wu_ref, cw_ref, wd_ref,
                o_ref, xbuf, hbuf, ua_buf, ub_buf, act_buf, *, tm, n_tiles):
    i = pl.program_id(1)
    _fill_halo_buf(xbuf, prev_ref, cur_ref, next_ref, i, n_tiles, tm)
    x = xbuf[...]
    r = lax.rsqrt(jnp.mean(x * x, axis=-1, keepdims=True) + EPS)
    h = (x * r * nw_ref[0]) * (1.0 + sc_ref[0]) + sh_ref[0]
    rows = lax.broadcasted_iota(jnp.int32, (tm + 2 * HALO, 1), 0)
    ok = jnp.logical_and(jnp.logical_or(rows >= HALO, i > 0),
                         jnp.logical_or(rows < HALO + tm, i < n_tiles - 1))
    hbuf[...] = jnp.where(ok, h, 0.0).astype(BF16)
    n_ff = D_FF // FF_TILE

    def up(j):
        hb = hbuf[...]
        ua_buf[j % 2] = _dot(hb, wu_ref[0, :, j * FF_TILE:(j + 1) * FF_TILE])
        ub_buf[j % 2] = _dot(hb, wu_ref[0, :, D_FF + j * FF_TILE:D_FF + (j + 1) * FF_TILE])

    up(0)
    for j in range(n_ff):
        if j + 1 < n_ff:
            up(j + 1)
        ca = slice(j * FF_TILE, (j + 1) * FF_TILE)
        cb = slice(D_FF + j * FF_TILE, D_FF + (j + 1) * FF_TILE)
        a = ua_buf[j % 2, pl.ds(HALO - 1, tm), :] * cw_ref[0, 0:1, ca]
        b = ub_buf[j % 2, pl.ds(HALO - 1, tm), :] * cw_ref[0, 0:1, cb]
        for k in range(1, FFN_CONV):
            off = HALO + k - FFN_CONV // 2
            a = a + ua_buf[j % 2, pl.ds(off, tm), :] * cw_ref[0, k:k + 1, ca]
            b = b + ub_buf[j % 2, pl.ds(off, tm), :] * cw_ref[0, k:k + 1, cb]
        act_buf[:, ca] = (_silu(a) * b).astype(BF16)
    o_ref[0] = cur_ref[0] + g_ref[0] * _dot(act_buf[...], wd_ref[0])


def _ffn_call(l, xs, mod, mod_row, norm_ffn, w_up, ffn_conv, w_down):
    B, rows, _ = xs.shape
    tm = _tile(rows)
    n = rows // tm
    prev, nxt = _halo_specs(tm, rows, D_MODEL)
    return pl.pallas_call(
        functools.partial(_ffn_kernel, tm=tm, n_tiles=n),
        grid=(B, n),
        in_specs=[pl.BlockSpec((1, tm, D_MODEL), lambda b, i: (b, i, 0)), prev, nxt,
                  _mod_spec(3, mod_row), _mod_spec(4, mod_row), _mod_spec(5, mod_row),
                  pl.BlockSpec((1, 1, D_MODEL), lambda b, i: (l, 0, 0)),
                  pl.BlockSpec((1, D_MODEL, 2 * D_FF), lambda b, i: (l, 0, 0), **_SINGLE),
                  pl.BlockSpec((1, FFN_CONV, 2 * D_FF), lambda b, i: (l, 0, 0)),
                  pl.BlockSpec((1, D_FF, D_MODEL), lambda b, i: (l, 0, 0), **_SINGLE)],
        out_specs=pl.BlockSpec((1, tm, D_MODEL), lambda b, i: (b, i, 0)),
        out_shape=jax.ShapeDtypeStruct(xs.shape, F32),
        scratch_shapes=[pltpu.VMEM((tm + 2 * HALO, D_MODEL), F32),
                        pltpu.VMEM((tm + 2 * HALO, D_MODEL), BF16),
                        pltpu.VMEM((2, tm + 2 * HALO, FF_TILE), F32),
                        pltpu.VMEM((2, tm + 2 * HALO, FF_TILE), F32),
                        pltpu.VMEM((tm, D_FF), BF16)],
        compiler_params=_params(("parallel", "parallel")),
        name="convffn",
    )(xs, xs, xs, mod, mod, mod, norm_ffn, w_up, ffn_conv, w_down)


def _rope_tables():
    half = GDN_DH // 2
    nf = half // 2
    t = jnp.arange(SEQ)
    freqs = ROPE_THETA ** (-jnp.arange(nf, dtype=F32) / nf)

    def part(pos):
        ang = pos.astype(F32)[:, None] * freqs
        cos, sin = jnp.cos(ang), jnp.sin(ang)
        return jnp.concatenate([cos, cos], axis=-1), jnp.concatenate([-sin, sin], axis=-1)

    c_r, s_r = part(t // GRID_W)
    c_c, s_c = part(t % GRID_W)
    return jnp.concatenate([c_r, c_c], axis=-1), jnp.concatenate([s_r, s_c], axis=-1)


def _pad_lanes(a, n=LANES):
    return jnp.pad(a, [(0, 0)] * (a.ndim - 1) + [(0, n - a.shape[-1])])


def kernel(x, c, ctx, c_ctx, w_ada, b_ada, norm_mix, w_in, pool_w, pool_scale, gdn_conv, gdn_a_log,
           gdn_dt_bias, gdn_norm, na_q_norm, na_k_norm, na_rpb, w_out, norm_ffn, w_up, ffn_conv, w_down):
    B = x.shape[0]
    depth = w_ada.shape[0]

    rows = -(-(B + 1) // SUBLANES) * SUBLANES
    cvec = jnp.concatenate([c, c_ctx[None, :], jnp.zeros((rows - B - 1, D_MODEL), F32)], axis=0)
    w_in_b = w_in.astype(BF16)
    wts = (w_in_b[:, :, :POOL_END], w_in_b[:, :, POOL_END:GQKV_END], w_in_b[:, :, GQKV_END:GZ_END],
           _pad_lanes(w_in_b[:, :, GZ_END:GAB_END]), w_in_b[:, :, GAB_END:])
    eye_g = jnp.eye(len(POOL_WINDOWS), dtype=F32)
    pool_bd = jnp.einsum("lgcd,gh->lgchd", pool_w, eye_g).reshape(depth, POOL_W, POOL_W).astype(BF16)
    alog = _pad_lanes(gdn_a_log.reshape(depth, 1, 2 * GDN_H))
    dtb = _pad_lanes(gdn_dt_bias.reshape(depth, 1, 2 * GDN_H))
    rope_tables = _rope_tables()
    qn_t = jnp.tile(na_q_norm, (1, NA_H)).reshape(depth, 1, NA_W)
    kn_t = jnp.tile(na_k_norm, (1, NA_H)).reshape(depth, 1, NA_W)
    w_out_b = w_out.astype(BF16)
    w_up_b = w_up.astype(BF16)
    w_down_b = w_down.astype(BF16)
    norm_mix3 = norm_mix.reshape(depth, 1, D_MODEL)
    norm_ffn3 = norm_ffn.reshape(depth, 1, D_MODEL)
    pool_scale3 = pool_scale.reshape(depth, 1, POOL_W)
    gdn_norm3 = gdn_norm.reshape(depth, 1, GDN_DH)
    s_zero = jnp.zeros((B, N_CHAIN, GDN_DH, GDN_DH), F32)
    na_bias = _na_bias_table(na_rpb)

    mods = _ada_call(cvec, w_ada, b_ada).reshape(depth, rows, 1, 6 * D_MODEL)

    for l in range(depth):
        mod = mods[l]
        with_ctx_out = l < depth - 1
        pv, gqkv, gz, gab, naq, nak, nav = _inproj_call(l, x, mod, None, norm_mix3, wts, qn_t, kn_t, gdn_conv,
                                                        rope_tables)
        cpv, cgqkv, cgz, cgab, cnaq, cnak, cnav = _inproj_call(l, ctx, mod, B, norm_mix3, wts, qn_t,
                                                               kn_t, gdn_conv, None)
        cprep = _gdn_prep_call(l, cgqkv, cgab, alog, dtb)
        co_f, co_b, s_ctx = _gdn_scan_call(cprep, s_zero)
        prep = _gdn_prep_call(l, gqkv, gab, alog, dtb)
        o_f, o_b, _ = _gdn_scan_call(prep, s_ctx)
        ya = _pool_call(l, pv, pool_bd, pool_scale3)
        yc = _na_call(l, naq, nak, nav, cnak, cnav, na_bias)
        x = _outproj_call(l, x, mod, None, ya, o_f, o_b, gz, yc, gdn_norm3, w_out_b, l > 0)
        x = _ffn_call(l, x, mod, None, norm_ffn3, w_up_b, ffn_conv, w_down_b)
        if with_ctx_out:
            cya = _pool_call(l, cpv, pool_bd, pool_scale3)
            cyc = _ctx_attn_call(cnaq, cnak, cnav)
            ctx = _outproj_call(l, ctx, mod, B, cya, co_f, co_b, cgz, cyc, gdn_norm3, w_out_b, l > 0)
            ctx = _ffn_call(l, ctx, mod, B, norm_ffn3, w_up_b, ffn_conv, w_down_b)
    return x
```

```python
import functools

import numpy as np
import jax
import jax.numpy as jnp
from jax import lax
from jax.experimental import pallas as pl
from jax.experimental.pallas import tpu as pltpu

F32 = jnp.float32
BF16 = jnp.bfloat16

D_MODEL = 1024
SEQ = 4096
CTX = 256
GRID_W = 64
EPS = 1e-6

POOL_W = 256
POOL_WINDOWS = (2, 4, 8, 16)
POOL_GD = 64
GDN_W = 512
GDN_H = 4
GDN_DH = 128
CHUNK = 64
SHORT_CONV = 5
ROPE_THETA = 10000.0
NA_W = 256
NA_H = 4
NA_DH = 64
NA_WIN_ROWS = 8
NA_WIN_COLS = 16
D_FF = 2816
FFN_CONV = 3

POOL_END = POOL_W
GQKV_END = POOL_END + 3 * GDN_W
GZ_END = GQKV_END + GDN_W
GAB_END = GZ_END + 4 * GDN_H

LANES = 128
SUBLANES = 8
HALO = SUBLANES
TM_MM = 512
TM_GDN = 256
FF_TILE = 256
FF_UP_CHUNKS = 3
NEG = -1e30
VMEM_LIMIT = 56 * 1024 * 1024

_HI = lax.Precision.HIGHEST
_SINGLE = dict(pipeline_mode=pl.Buffered(1))


def _dot(a, b):
    return jnp.dot(a, b, preferred_element_type=F32)


def _silu(x):
    return x * jax.nn.sigmoid(x)


def _params(sem, vmem=VMEM_LIMIT):
    return pltpu.CompilerParams(dimension_semantics=sem, vmem_limit_bytes=vmem)


def _tile(rows):
    return min(rows, TM_MM)


def _mod_spec(chunk, row):
    return pl.BlockSpec((1, 1, D_MODEL), lambda b, i: (b if row is None else row, 0, chunk))


def _halo_specs(tm, rows, width, col=None):
    nb = tm // HALO
    last = rows // HALO - 1
    c = 0 if col is None else col
    prev = pl.BlockSpec((1, HALO, width), lambda b, i: (b, jnp.maximum(i * nb - 1, 0), c))
    nxt = pl.BlockSpec((1, HALO, width), lambda b, i: (b, jnp.minimum((i + 1) * nb, last), c))
    return prev, nxt


def _fill_halo_buf(buf, prev_ref, cur_ref, next_ref, i, n_tiles, tm):
    buf[0:HALO, :] = jnp.where(i > 0, prev_ref[0], 0.0)
    buf[HALO:HALO + tm, :] = cur_ref[0]
    buf[HALO + tm:2 * HALO + tm, :] = jnp.where(i < n_tiles - 1, next_ref[0], 0.0)


def _ada_kernel(c_ref, w_ref, b_ref, o_ref):
    sc = _silu(c_ref[...]).astype(BF16)
    o_ref[0] = _dot(sc, w_ref[0].astype(BF16)) + b_ref[0]


def _ada_call(cvec, w_ada, b_ada):
    depth = w_ada.shape[0]
    rows = cvec.shape[0]
    tn = 1536
    return pl.pallas_call(
        _ada_kernel,
        grid=(depth, 6 * D_MODEL // tn),
        in_specs=[
            pl.BlockSpec((rows, D_MODEL), lambda l, j: (0, 0)),
            pl.BlockSpec((1, D_MODEL, tn), lambda l, j: (l, 0, j)),
            pl.BlockSpec((1, 1, tn), lambda l, j: (l, 0, j)),
        ],
        out_specs=pl.BlockSpec((1, rows, tn), lambda l, j: (l, 0, j)),
        out_shape=jax.ShapeDtypeStruct((depth, rows, 6 * D_MODEL), F32),
        compiler_params=_params(("parallel", "parallel")),
        name="adaln",
    )(cvec, w_ada, b_ada.reshape(depth, 1, 6 * D_MODEL))


def _inproj_kernel(*refs, tm, n_tiles, rope):
    if rope:
        (x_ref, xp_ref, xn_ref, sh_ref, sc_ref, nw_ref, wp_ref, wq_ref, wz_ref, wab_ref, wna_ref,
         qn_ref, kn_ref, cw_ref, cos_ref, sin_ref,
         pool_o, gqkv_o, gz_o, gab_o, naq_o, nak_o, nav_o, xbuf) = refs
    else:
        (x_ref, xp_ref, xn_ref, sh_ref, sc_ref, nw_ref, wp_ref, wq_ref, wz_ref, wab_ref, wna_ref,
         qn_ref, kn_ref, cw_ref,
         pool_o, gqkv_o, gz_o, gab_o, naq_o, nak_o, nav_o, xbuf) = refs
    i = pl.program_id(1)
    _fill_halo_buf(xbuf, xp_ref, x_ref, xn_ref, i, n_tiles, tm)
    x = xbuf[...]
    r = lax.rsqrt(jnp.mean(x * x, axis=-1, keepdims=True) + EPS)
    h = (x * r * nw_ref[0]) * (1.0 + sc_ref[0]) + sh_ref[0]
    rows = lax.broadcasted_iota(jnp.int32, (tm + 2 * HALO, 1), 0)
    ok = jnp.logical_and(jnp.logical_or(rows >= HALO, i > 0),
                         jnp.logical_or(rows < HALO + tm, i < n_tiles - 1))
    hb_full = jnp.where(ok, h, 0.0).astype(BF16)
    hb = hb_full[HALO:HALO + tm]

    p_qkv = _dot(hb_full, wq_ref[0])
    nb = tm + 2 * HALO
    acc = None
    for k in range(SHORT_CONV):
        shift = (SHORT_CONV // 2 - k) % nb
        xs = p_qkv if shift == 0 else pltpu.roll(p_qkv, shift, axis=0)
        term = xs[HALO:HALO + tm] * cw_ref[0, k:k + 1, :]
        acc = term if acc is None else acc + term
    qkv = _silu(acc)
    lane = lax.broadcasted_iota(jnp.int32, (tm, GDN_DH), 1)
    first = (lane % (GDN_DH // 2)) < (GDN_DH // 4)

    def norm_rope(xv):
        xn = xv * lax.rsqrt(jnp.sum(xv * xv, axis=-1, keepdims=True) + EPS)
        if not rope:
            return xn
        partner = jnp.where(first, pltpu.roll(xn, GDN_DH - GDN_DH // 4, axis=1),
                            pltpu.roll(xn, GDN_DH // 4, axis=1))
        return xn * cos_ref[...] + partner * sin_ref[...]

    for hd in range(GDN_H):
        cq = slice(hd * GDN_DH, (hd + 1) * GDN_DH)
        ck = slice(GDN_W + hd * GDN_DH, GDN_W + (hd + 1) * GDN_DH)
        gqkv_o[0, :, cq] = norm_rope(qkv[:, cq]) * (GDN_DH ** -0.5)
        gqkv_o[0, :, ck] = norm_rope(qkv[:, ck])
    gqkv_o[0, :, 2 * GDN_W:] = qkv[:, 2 * GDN_W:]

    pool_o[0] = _dot(hb, wp_ref[0])
    gz_o[0] = _dot(hb, wz_ref[0])
    gab_o[0] = _dot(hb, wab_ref[0])
    na = _dot(hb, wna_ref[0])
    li = lax.broadcasted_iota(jnp.int32, (NA_W, NA_W), 0) // NA_DH
    lj = lax.broadcasted_iota(jnp.int32, (NA_W, NA_W), 1) // NA_DH
    seg = (li == lj).astype(BF16)
    q = na[:, 0:NA_W]
    k = na[:, NA_W:2 * NA_W]

    def seg_mean(sq):
        hi = sq.astype(BF16)
        lo = (sq - hi.astype(F32)).astype(BF16)
        return (_dot(hi, seg) + _dot(lo, seg)) * (1.0 / NA_DH)

    qs = seg_mean(q * q)
    ks = seg_mean(k * k)
    qn = q * lax.rsqrt(qs + EPS) * qn_ref[0] * (NA_DH ** -0.5)
    kn = k * lax.rsqrt(ks + EPS) * kn_ref[0]
    v = na[:, 2 * NA_W:3 * NA_W]

    def with_swizzle(t):
        return jnp.concatenate([t, pltpu.roll(t, NA_DH, axis=1)], axis=1).astype(BF16)

    naq_o[0] = with_swizzle(qn)
    nak_o[0] = with_swizzle(kn)
    nav_o[0] = with_swizzle(v)


def _inproj_call(l, xs, mod, mod_row, norm_mix, wts, qn_t, kn_t, conv_w, rope_tables):
    B, rows, _ = xs.shape
    tm = _tile(rows)
    n = rows // tm
    rope = rope_tables is not None
    prev, nxt = _halo_specs(tm, rows, D_MODEL)

    def wspec(w):
        return pl.BlockSpec((1,) + w.shape[1:], lambda b, i: (l, 0, 0), **_SINGLE)

    def rowspec(width):
        return pl.BlockSpec((1, tm, width), lambda b, i: (b, i, 0))

    def vec(width, sub=1):
        return pl.BlockSpec((1, sub, width), lambda b, i: (l, 0, 0))

    in_specs = ([rowspec(D_MODEL), prev, nxt, _mod_spec(0, mod_row), _mod_spec(1, mod_row), vec(D_MODEL)]
                + [wspec(w) for w in wts] + [vec(NA_W), vec(NA_W), vec(3 * GDN_W, SHORT_CONV)])
    args = [xs, xs, xs, mod, mod, norm_mix, *wts, qn_t, kn_t, conv_w]
    if rope:
        in_specs += [pl.BlockSpec((tm, GDN_DH), lambda b, i: (i, 0))] * 2
        args += list(rope_tables)
    outs = [(POOL_W, F32), (3 * GDN_W, F32), (GDN_W, F32), (LANES, F32),
            (2 * NA_W, BF16), (2 * NA_W, BF16), (2 * NA_W, BF16)]
    return pl.pallas_call(
        functools.partial(_inproj_kernel, tm=tm, n_tiles=n, rope=rope),
        grid=(B, n),
        in_specs=in_specs,
        out_specs=[rowspec(width) for width, _ in outs],
        out_shape=[jax.ShapeDtypeStruct((B, rows, width), dt) for width, dt in outs],
        scratch_shapes=[pltpu.VMEM((tm + 2 * HALO, D_MODEL), F32)],
        compiler_params=_params(("parallel", "parallel")),
        name="inproj",
    )(*args)


def _pool_kernel(cur_ref, prev_ref, next_ref, w_ref, s_ref, o_ref, buf, *, tm, rows):
    i = pl.program_id(1)
    _fill_halo_buf(buf, prev_ref, cur_ref, next_ref, i, rows // tm, tm)
    v = cur_ref[0]
    t = lax.broadcasted_iota(jnp.int32, (tm, POOL_W), 0) + i * tm
    lane = lax.broadcasted_iota(jnp.int32, (tm, POOL_W), 1)
    half = jnp.left_shift(1, lane // POOL_GD)
    cnt = (jnp.minimum(t + half, rows) - jnp.maximum(t - half, 0)).astype(F32)

    nb = tm + 2 * HALO
    xb = buf[...]

    def shifted(a, k):
        return pltpu.roll(a, k % nb, axis=0)

    a2 = xb + shifted(xb, 1)
    a4 = shifted(a2, 1) + shifted(a2, -1)
    a8 = shifted(a4, 2) + shifted(a4, -2)
    a16 = shifted(a8, 4) + shifted(a8, -4)
    grp = lane // POOL_GD
    pooled = jnp.where(grp == 0, a2[HALO:HALO + tm],
                       jnp.where(grp == 1, a4[HALO:HALO + tm],
                                 jnp.where(grp == 2, a8[HALO:HALO + tm], a16[HALO:HALO + tm])))
    pooled = pooled / cnt - v
    o_ref[0] = (_dot(pooled.astype(BF16), w_ref[0]) * s_ref[0]).astype(BF16)


def _pool_call(l, pv, w_bd, scale):
    B, rows, _ = pv.shape
    tm = _tile(rows)
    prev, nxt = _halo_specs(tm, rows, POOL_W)
    return pl.pallas_call(
        functools.partial(_pool_kernel, tm=tm, rows=rows),
        grid=(B, rows // tm),
        in_specs=[
            pl.BlockSpec((1, tm, POOL_W), lambda b, i: (b, i, 0)), prev, nxt,
            pl.BlockSpec((1, POOL_W, POOL_W), lambda b, i: (l, 0, 0)),
            pl.BlockSpec((1, 1, POOL_W), lambda b, i: (l, 0, 0)),
        ],
        out_specs=pl.BlockSpec((1, tm, POOL_W), lambda b, i: (b, i, 0)),
        out_shape=jax.ShapeDtypeStruct((B, rows, POOL_W), BF16),
        scratch_shapes=[pltpu.VMEM((tm + 2 * HALO, POOL_W), F32)],
        compiler_params=_params(("parallel", "parallel")),
        name="pool",
    )(pv, pv, pv, w_bd, scale)


CPT = TM_GDN // CHUNK
N_CHAIN = 2 * GDN_H
SCAN_NB = 2


def _bdot(a, b):
    return lax.dot_general(a.astype(BF16), b.astype(BF16), (((2,), (1,)), ((0,), (0,))),
                           preferred_element_type=F32)


def _bdot_nt(a, b):
    return lax.dot_general(a.astype(BF16), b.astype(BF16), (((2,), (2,)), ((0,), (0,))),
                           preferred_element_type=F32)


def _split3(x):
    hi = x.astype(BF16)
    r1 = x - hi.astype(F32)
    mid = r1.astype(BF16)
    lo = (r1 - mid.astype(F32)).astype(BF16)
    return hi, mid, lo


def _gdn_prep_kernel(qkv_ref, gab_ref, alog_ref, dtb_ref,
                     uf_o, ub_o, wqf_o, wqb_o, ikf_o, ikb_o, gl_o, gt_buf):
    tm = TM_GDN
    qkv = qkv_ref[0]

    gab = gab_ref[0]
    g_all = -jnp.exp(alog_ref[0]) * jax.nn.softplus(gab + dtb_ref[0])
    beta_all = jax.nn.sigmoid(gab)
    ri = lax.broadcasted_iota(jnp.int32, (tm, tm), 0)
    rj = lax.broadcasted_iota(jnp.int32, (tm, tm), 1)
    same = (ri // CHUNK) == (rj // CHUNK)
    low = jnp.logical_and(same, rj <= ri).astype(BF16)
    upp = jnp.logical_and(same, rj >= ri).astype(BF16)
    g_hi, g_mid, g_lo = _split3(g_all)
    pre = _dot(low, g_hi) + _dot(low, g_mid) + _dot(low, g_lo)
    suf = _dot(upp, g_hi) + _dot(upp, g_mid) + _dot(upp, g_lo)
    glane = lax.broadcasted_iota(jnp.int32, (tm, LANES), 1)
    gc_all = jnp.where(glane < GDN_H, pre, suf)
    gt_buf[...] = gc_all.T

    pi = lax.broadcasted_iota(jnp.int32, (CHUNK, 2 * CHUNK), 0)
    pj = lax.broadcasted_iota(jnp.int32, (CHUNK, 2 * CHUNK), 1)
    is_fwd = pj < CHUNK
    pjj = jnp.where(is_fwd, pj, pj - CHUNK)
    sgn = jnp.where(is_fwd, 1, -1)
    incl = (pi - pjj) * sgn >= 0
    strict = (pi - pjj) * sgn > 0

    fwd_half = jnp.where(is_fwd, 1.0, 0.0).astype(BF16)
    bwd_half = jnp.where(is_fwd, 0.0, 1.0).astype(BF16)

    def block_diag(x):
        xb16 = x.astype(BF16)
        return jnp.concatenate([xb16 * fwd_half, xb16 * bwd_half], axis=1)

    lhs_l, k2_l, decay_l, beta_l, rhs_l, qd_l, kd_l, gl_l = [], [], [], [], [], [], [], []
    for h in range(GDN_H):
        q3 = qkv[:, h * GDN_DH:(h + 1) * GDN_DH].reshape(CPT, CHUNK, GDN_DH)
        k3 = qkv[:, GDN_W + h * GDN_DH:GDN_W + (h + 1) * GDN_DH].reshape(CPT, CHUNK, GDN_DH)
        v3 = qkv[:, 2 * GDN_W + h * GDN_DH:2 * GDN_W + (h + 1) * GDN_DH].reshape(CPT, CHUNK, GDN_DH)
        kb16 = k3.astype(BF16)
        lhs_l.append(jnp.concatenate([kb16, q3.astype(BF16)], axis=1))
        k2_l.append(jnp.concatenate([kb16, kb16], axis=1))
        gcols, betas, grows, rhs_d = [], [], [], []
        for d in range(2):
            gi = d * GDN_H + h
            bi = 2 * GDN_H + gi
            gcol = jnp.broadcast_to(gc_all[:, gi:gi + 1], (tm, GDN_DH)).reshape(CPT, CHUNK, GDN_DH)
            beta = jnp.broadcast_to(beta_all[:, bi:bi + 1], (tm, GDN_DH)).reshape(CPT, CHUNK, GDN_DH)
            grow_full = gt_buf[gi:gi + 1, :]
            grows.append([grow_full[:, c * CHUNK:(c + 1) * CHUNK] for c in range(CPT)])
            e = jnp.exp(gcol)
            rhs_d.append(jnp.concatenate([v3 * beta, k3 * (beta * e)], axis=2))
            g_last = gcol[:, 0:1, :] if d else gcol[:, CHUNK - 1:CHUNK, :]
            qd_l.append(q3 * e)
            kd_l.append(k3 * jnp.exp(g_last - gcol))
            gl_l.append(jnp.exp(g_last))
            gcols.append(gcol)
            betas.append(beta)
        grow = jnp.stack([jnp.concatenate([grows[0][c], grows[1][c]], axis=1) for c in range(CPT)],
                         axis=0)
        diff = jnp.where(is_fwd, gcols[0], gcols[1]) - grow
        decay_l.append(jnp.where(incl, jnp.exp(jnp.where(incl, diff, 0.0)), 0.0))
        beta_l.append(jnp.where(is_fwd, betas[0], betas[1]))
        rhs_l.append(jnp.concatenate(rhs_d, axis=1))

    decay = jnp.concatenate(decay_l, axis=0)
    beta_p = jnp.concatenate(beta_l, axis=0)
    m1 = _bdot_nt(jnp.concatenate(lhs_l, axis=0), jnp.concatenate(k2_l, axis=0))
    a_mat = jnp.where(strict, m1[:, :CHUNK] * (decay * beta_p), 0.0)
    intra = m1[:, CHUNK:] * decay
    p = -a_mat
    t_off = p
    for _ in range(5):
        p = _bdot(p, block_diag(p))
        t_off = t_off + p + _bdot(t_off, block_diag(p))
    rhs = jnp.concatenate(rhs_l, axis=0)
    uw = rhs + _bdot(block_diag(t_off), rhs)

    outs = ((uf_o, wqf_o, ikf_o), (ub_o, wqb_o, ikb_o))
    for d in range(2):
        u_o, wq_o, ik_o = outs[d]
        prow = slice(d * CHUNK, (d + 1) * CHUNK)
        for h in range(GDN_H):
            n0 = h * CPT
            cols = slice(h * GDN_DH, (h + 1) * GDN_DH)
            u_o[0, :, cols] = uw[n0:n0 + CPT, prow, :GDN_DH].reshape(tm, GDN_DH).astype(BF16)
            wq_o[0, h, :, 0:CHUNK, :] = uw[n0:n0 + CPT, prow, GDN_DH:].astype(BF16)
            wq_o[0, h, :, CHUNK:, :] = qd_l[2 * h + d].astype(BF16)
            ik_o[0, h, :, 0:CHUNK, :] = intra[n0:n0 + CPT, :, prow].astype(BF16)
            kd3 = kd_l[2 * h + d]
            for c in range(CPT):
                ik_o[0, h, c, CHUNK:, :] = kd3[c].T.astype(BF16)
            gl_o[0, h, :, d:d + 1, :] = gl_l[2 * h + d]


def _gdn_prep_call(l, qkv, gab, alog, dtb):
    B, rows, _ = qkv.shape
    tm = TM_GDN
    n = rows // tm

    def vec(width):
        return pl.BlockSpec((1, 1, width), lambda b, i: (l, 0, 0))

    in_specs = [pl.BlockSpec((1, tm, 3 * GDN_W), lambda b, i: (b, i, 0)),
                pl.BlockSpec((1, tm, LANES), lambda b, i: (b, i, 0)), vec(LANES), vec(LANES)]
    args = [qkv, gab, alog, dtb]
    nc = rows // CHUNK
    u_spec = pl.BlockSpec((1, tm, GDN_W), lambda b, i: (b, i, 0))
    wq_spec = pl.BlockSpec((1, GDN_H, CPT, 2 * CHUNK, GDN_DH), lambda b, i: (b, 0, i, 0, 0))
    ik_spec = pl.BlockSpec((1, GDN_H, CPT, 3 * CHUNK, CHUNK), lambda b, i: (b, 0, i, 0, 0))
    gl_spec = pl.BlockSpec((1, GDN_H, CPT, 2, LANES), lambda b, i: (b, 0, i, 0, 0))
    u_sh = jax.ShapeDtypeStruct((B, rows, GDN_W), BF16)
    wq_sh = jax.ShapeDtypeStruct((B, GDN_H, nc, 2 * CHUNK, GDN_DH), BF16)
    ik_sh = jax.ShapeDtypeStruct((B, GDN_H, nc, 3 * CHUNK, CHUNK), BF16)
    gl_sh = jax.ShapeDtypeStruct((B, GDN_H, nc, 2, LANES), F32)
    return pl.pallas_call(
        _gdn_prep_kernel,
        grid=(B, n),
        in_specs=in_specs,
        out_specs=[u_spec, u_spec, wq_spec, wq_spec, ik_spec, ik_spec, gl_spec],
        out_shape=[u_sh, u_sh, wq_sh, wq_sh, ik_sh, ik_sh, gl_sh],
        scratch_shapes=[pltpu.VMEM((LANES, tm), F32)],
        compiler_params=_params(("parallel", "parallel")),
        name="gdn_prep",
    )(*args)


def _gdn_scan_kernel(uf_ref, ub_ref, wqf_ref, wqb_ref, ikf_ref, ikb_ref, glf_ref, glb_ref, s0_ref,
                     of_o, ob_o, s_o, s_ref, *, n_tiles, nb):
    i = pl.program_id(1)

    @pl.when(i == 0)
    def _():
        s_ref[...] = s0_ref[...]

    refs = ((uf_ref, wqf_ref, ikf_ref, glf_ref, of_o), (ub_ref, wqb_ref, ikb_ref, glb_ref, ob_o))
    chains = [(e, d, h) for e in range(nb) for d in range(2) for h in range(GDN_H)]
    states = [s_ref[e, d * GDN_H + h] for e, d, h in chains]
    for step in range(CPT):
        ws_qs = []
        for n, (e, d, h) in enumerate(chains):
            c = CPT - 1 - step if d else step
            ws_qs.append(_dot(refs[d][1][e, h, c], states[n].astype(BF16)))
        vb = []
        for n, (e, d, h) in enumerate(chains):
            c = CPT - 1 - step if d else step
            u = refs[d][0][e, c * CHUNK:(c + 1) * CHUNK, h * GDN_DH:(h + 1) * GDN_DH]
            vb.append((u - ws_qs[n][:CHUNK]).astype(BF16))
        r = []
        for n, (e, d, h) in enumerate(chains):
            c = CPT - 1 - step if d else step
            r.append(_dot(refs[d][2][e, h, c], vb[n]))
        for n, (e, d, h) in enumerate(chains):
            c = CPT - 1 - step if d else step
            refs[d][4][e, c * CHUNK:(c + 1) * CHUNK, h * GDN_DH:(h + 1) * GDN_DH] = (
                ws_qs[n][CHUNK:] + r[n][:CHUNK]).astype(BF16)
            states[n] = states[n] * refs[d][3][e, h, c, d:d + 1, :] + r[n][CHUNK:]
    for n, (e, d, h) in enumerate(chains):
        s_ref[e, d * GDN_H + h] = states[n]

    @pl.when(i == n_tiles - 1)
    def _():
        s_o[...] = s_ref[...]


def _gdn_scan_call(prep, s0):
    uf, ub, wqf, wqb, ikf, ikb, gl = prep
    B, rows, _ = uf.shape
    tm = TM_GDN
    n = rows // tm
    nb = SCAN_NB if B % SCAN_NB == 0 else 1

    def ident(i):
        return i

    def rev(i):
        return n - 1 - i

    def rowspec(order):
        return pl.BlockSpec((nb, tm, GDN_W), lambda b, i: (b, order(i), 0))

    def tilespec(arr, order):
        return pl.BlockSpec((nb, GDN_H, CPT) + arr.shape[3:], lambda b, i: (b, 0, order(i), 0, 0))

    s_spec = pl.BlockSpec((nb, N_CHAIN, GDN_DH, GDN_DH), lambda b, i: (b, 0, 0, 0))
    o_shape = jax.ShapeDtypeStruct((B, rows, GDN_W), BF16)
    return pl.pallas_call(
        functools.partial(_gdn_scan_kernel, n_tiles=n, nb=nb),
        grid=(B // nb, n),
        in_specs=[rowspec(ident), rowspec(rev), tilespec(wqf, ident), tilespec(wqb, rev),
                  tilespec(ikf, ident), tilespec(ikb, rev), tilespec(gl, ident), tilespec(gl, rev),
                  s_spec],
        out_specs=[rowspec(ident), rowspec(rev), s_spec],
        out_shape=[o_shape, o_shape, jax.ShapeDtypeStruct(s0.shape, F32)],
        scratch_shapes=[pltpu.VMEM((nb, N_CHAIN, GDN_DH, GDN_DH), F32)],
        compiler_params=_params(("parallel", "arbitrary")),
        name="gdn_scan",
    )(uf, ub, wqf, wqb, ikf, ikb, gl, gl, s0)


NA_ROWS = SEQ // GRID_W
NA_BAND = NA_WIN_ROWS * GRID_W
NA_RPS = 8
_NT = (((1,), (1,)), ((), ()))


def _na_head_cols(h):
    off = (h // 2) * LANES if h % 2 == 0 else NA_W + ((h // 2 + 1) % (NA_H // 2)) * LANES
    return slice(off, off + NA_DH)


def _lane_fold(parts, op):
    tiles = [p[:, t * LANES:(t + 1) * LANES] for p in parts for t in range(p.shape[1] // LANES)]
    acc = tiles[0]
    for t in tiles[1:]:
        acc = op(acc, t)
    return acc


def _na_kernel(q_ref, k_ref, v_ref, kc_ref, vc_ref, bias_ref, o_ref):
    step = pl.program_id(1)
    q = q_ref[0]
    kc = kc_ref[0]
    vc = vc_ref[0]
    heads = [_na_head_cols(h) for h in range(NA_H)]
    qrows = [slice(j * GRID_W, (j + 1) * GRID_W) for j in range(NA_RPS)]
    kb, vb, d0 = [], [], []
    for j in range(NA_RPS):
        r = step * NA_RPS + j
        r0 = jnp.clip(r - NA_WIN_ROWS // 2, 0, NA_ROWS - NA_WIN_ROWS)
        d0.append(r0 - r + NA_WIN_ROWS - 1)
        start = pl.multiple_of(r0 * GRID_W, GRID_W)
        kb.append(k_ref[0, pl.ds(start, NA_BAND), :])
        vb.append(v_ref[0, pl.ds(start, NA_BAND), :])

    def scores(j):
        return [lax.dot_general(q[qrows[j], c], kb[j][:, c], _NT, preferred_element_type=F32)
                for c in heads]

    s_ctx = [lax.dot_general(q[:, c], kc[:, c], _NT, preferred_element_type=F32) for c in heads]
    p_ctx = [[None] * NA_RPS for _ in range(NA_H)]
    den = [[None] * NA_H for _ in range(NA_RPS)]
    o_loc = [None] * NA_RPS
    s_next = scores(0)
    for j in range(NA_RPS):
        s_loc = s_next
        if j + 1 < NA_RPS:
            s_next = scores(j + 1)
        p_loc = []
        for h in range(NA_H):
            sl = s_loc[h] + bias_ref[h, d0[j]]
            sc = s_ctx[h][qrows[j]]
            m = jnp.max(_lane_fold([sl, sc], jnp.maximum), axis=-1, keepdims=True)
            pl_ = jnp.exp(sl - m)
            pc_ = jnp.exp(sc - m)
            den[j][h] = jnp.sum(_lane_fold([pl_, pc_], jnp.add), axis=-1, keepdims=True)
            p_loc.append(pl_.astype(BF16))
            p_ctx[h][j] = pc_.astype(BF16)
        o_loc[j] = [_dot(p_loc[h], vb[j][:, c]) for h, c in enumerate(heads)]
    o_ctx = [_dot(jnp.concatenate(p_ctx[h], axis=0), vc[:, c]) for h, c in enumerate(heads)]
    for j in range(NA_RPS):
        o_ref[0, qrows[j], :] = jnp.concatenate(
            [(o_loc[j][h] + o_ctx[h][qrows[j]]) / den[j][h] for h in range(NA_H)], axis=-1).astype(BF16)


def _na_call(l, q, k, v, kc, vc, bias):
    B = q.shape[0]
    full = pl.BlockSpec((1, SEQ, 2 * NA_W), lambda b, r: (b, 0, 0))
    cfull = pl.BlockSpec((1, CTX, 2 * NA_W), lambda b, r: (b, 0, 0))
    blk = pl.BlockSpec((1, NA_RPS * GRID_W, 2 * NA_W), lambda b, r: (b, r, 0))
    oblk = pl.BlockSpec((1, NA_RPS * GRID_W, NA_W), lambda b, r: (b, r, 0))
    return pl.pallas_call(
        _na_kernel,
        grid=(B, NA_ROWS // NA_RPS),
        in_specs=[blk, full, full, cfull, cfull,
                  pl.BlockSpec((NA_H,) + bias.shape[1:], lambda b, r: (l, 0, 0, 0))],
        out_specs=oblk,
        out_shape=jax.ShapeDtypeStruct((B, SEQ, NA_W), BF16),
        compiler_params=_params(("parallel", "arbitrary")),
        name="natten",
    )(q, k, v, kc, vc, bias)


def _ctx_attn_kernel(q_ref, k_ref, v_ref, o_ref):
    q, k, v = q_ref[0], k_ref[0], v_ref[0]
    outs = []
    for h in range(NA_H):
        cols = _na_head_cols(h)
        s = lax.dot_general(q[:, cols], k[:, cols], _NT, preferred_element_type=F32)
        p = jnp.exp(s - jnp.max(s, axis=-1, keepdims=True))
        den = jnp.sum(p, axis=-1, keepdims=True)
        outs.append(_dot(p.astype(BF16), v[:, cols]) / den)
    o_ref[0] = jnp.concatenate(outs, axis=-1).astype(BF16)


def _ctx_attn_call(q, k, v):
    B = q.shape[0]
    spec = pl.BlockSpec((1, CTX, 2 * NA_W), lambda b: (b, 0, 0))
    ospec = pl.BlockSpec((1, CTX, NA_W), lambda b: (b, 0, 0))
    return pl.pallas_call(
        _ctx_attn_kernel,
        grid=(B,),
        in_specs=[spec, spec, spec],
        out_specs=ospec,
        out_shape=jax.ShapeDtypeStruct((B, CTX, NA_W), BF16),
        compiler_params=_params(("parallel",)),
        name="ctx_attn",
    )(q, k, v)


def _na_bias_table(rpb):
    depth = rpb.shape[0]
    qc = np.arange(GRID_W)[:, None]
    kc = np.arange(GRID_W)[None, :]
    c0 = np.clip(qc - NA_WIN_COLS // 2, 0, GRID_W - NA_WIN_COLS)
    mask = (kc >= c0) & (kc < c0 + NA_WIN_COLS)
    dc = kc - qc + NA_WIN_COLS - 1
    n_dr, n_dc = 2 * NA_WIN_ROWS - 1, 2 * NA_WIN_COLS - 1
    sel_c = (np.arange(n_dc)[:, None, None] == dc[None]) & mask[None]
    dr = np.arange(NA_WIN_ROWS)[:, None] + np.arange(NA_WIN_ROWS)[None, :]
    sel_r = np.arange(n_dr)[:, None, None] == dr[None]
    t = jnp.einsum("lhrc,rdk,cqn->lhdqkn", rpb, jnp.asarray(sel_r, F32), jnp.asarray(sel_c, F32),
                   precision=_HI)
    t = jnp.where(mask[:, None, :], t, NEG)
    return t.reshape(depth * NA_H, NA_WIN_ROWS, GRID_W, NA_BAND)


def _outproj_kernel(x_ref, ya_ref, of_ref, ob_ref, z_ref, yc_ref, g_ref, gn_ref, w_ref, o_ref):
    acc = _dot(ya_ref[0], w_ref[0, 0:POOL_W, :])
    o = of_ref[0].astype(F32) + ob_ref[0].astype(F32)
    z = z_ref[0]
    for h in range(GDN_H):
        cols = slice(h * GDN_DH, (h + 1) * GDN_DH)
        oh = o[:, cols]
        on = oh * lax.rsqrt(jnp.mean(oh * oh, axis=-1, keepdims=True) + EPS) * gn_ref[0]
        yb = (on * _silu(z[:, cols])).astype(BF16)
        acc = acc + _dot(yb, w_ref[0, POOL_W + h * GDN_DH:POOL_W + (h + 1) * GDN_DH, :])
    acc = acc + _dot(yc_ref[0], w_ref[0, POOL_W + GDN_W:, :])
    o_ref[0] = x_ref[0] + g_ref[0] * acc


def _outproj_call(l, xs, mod, mod_row, ya, o_f, o_b, gz, yc, gdn_norm, w_out, in_place):
    B, rows, _ = xs.shape
    tm = _tile(rows)

    def rowspec(n):
        return pl.BlockSpec((1, tm, n), lambda b, i: (b, i, 0))

    return pl.pallas_call(
        _outproj_kernel,
        grid=(B, rows // tm),
        in_specs=[rowspec(D_MODEL), rowspec(POOL_W), rowspec(GDN_W), rowspec(GDN_W), rowspec(GDN_W),
                  rowspec(NA_W), _mod_spec(2, mod_row),
                  pl.BlockSpec((1, 1, GDN_DH), lambda b, i: (l, 0, 0)),
                  pl.BlockSpec((1, D_MODEL, D_MODEL), lambda b, i: (l, 0, 0), **_SINGLE)],
        out_specs=rowspec(D_MODEL),
        out_shape=jax.ShapeDtypeStruct(xs.shape, F32),
        input_output_aliases={0: 0} if in_place else {},
        compiler_params=_params(("parallel", "parallel")),
        name="outproj",
    )(xs, ya, o_f, o_b, gz, yc, mod, gdn_norm, w_out)


def _ffn_kernel(cur_ref, prev_ref, next_ref, sh_ref, sc_ref, g_ref, nw_ref, wu_ref, cw_ref, wd_ref,
                o_ref, xbuf, hbuf, ua_buf, ub_buf, act_buf, *, tm, n_tiles):
    i = pl.program_id(1)
    _fill_halo_buf(xbuf, prev_ref, cur_ref, next_ref, i, n_tiles, tm)
    x = xbuf[...]
    r = lax.rsqrt(jnp.mean(x * x, axis=-1, keepdims=True) + EPS)
    h = (x * r * nw_ref[0]) * (1.0 + sc_ref[0]) + sh_ref[0]
    rows = lax.broadcasted_iota(jnp.int32, (tm + 2 * HALO, 1), 0)
    ok = jnp.logical_and(jnp.logical_or(rows >= HALO, i > 0),
                         jnp.logical_or(rows < HALO + tm, i < n_tiles - 1))
    hbuf[...] = jnp.where(ok, h, 0.0).astype(BF16)
    n_ff = D_FF // FF_TILE
    n_up = FF_UP_CHUNKS if (tm + 2 * HALO) % (FF_UP_CHUNKS * 2 * SUBLANES) == 0 else 1
    up_rows = (tm + 2 * HALO) // n_up

    def up(j):
        wa = wu_ref[0, :, j * FF_TILE:(j + 1) * FF_TILE]
        wb = wu_ref[0, :, D_FF + j * FF_TILE:D_FF + (j + 1) * FF_TILE]
        for m in range(n_up):
            rs = slice(m * up_rows, (m + 1) * up_rows)
            ua_buf[j % 2, rs, :] = _dot(hbuf[rs, :], wa)
            ub_buf[j % 2, rs, :] = _dot(hbuf[rs, :], wb)

    up(0)
    for j in range(n_ff):
        if j + 1 < n_ff:
            up(j + 1)
        ca = slice(j * FF_TILE, (j + 1) * FF_TILE)
        cb = slice(D_FF + j * FF_TILE, D_FF + (j + 1) * FF_TILE)
        a = ua_buf[j % 2, pl.ds(HALO - 1, tm), :] * cw_ref[0, 0:1, ca]
        b = ub_buf[j % 2, pl.ds(HALO - 1, tm), :] * cw_ref[0, 0:1, cb]
        for k in range(1, FFN_CONV):
            off = HALO + k - FFN_CONV // 2
            a = a + ua_buf[j % 2, pl.ds(off, tm), :] * cw_ref[0, k:k + 1, ca]
            b = b + ub_buf[j % 2, pl.ds(off, tm), :] * cw_ref[0, k:k + 1, cb]
        act_buf[:, ca] = (_silu(a) * b).astype(BF16)
    for m in range(2):
        rs = slice(m * (tm // 2), (m + 1) * (tm // 2))
        o_ref[0, rs, :] = cur_ref[0, rs, :] + g_ref[0] * _dot(act_buf[rs, :], wd_ref[0])


def _ffn_call(l, xs, mod, mod_row, norm_ffn, w_up, ffn_conv, w_down):
    B, rows, _ = xs.shape
    tm = _tile(rows)
    n = rows // tm
    prev, nxt = _halo_specs(tm, rows, D_MODEL)
    return pl.pallas_call(
        functools.partial(_ffn_kernel, tm=tm, n_tiles=n),
        grid=(B, n),
        in_specs=[pl.BlockSpec((1, tm, D_MODEL), lambda b, i: (b, i, 0)), prev, nxt,
                  _mod_spec(3, mod_row), _mod_spec(4, mod_row), _mod_spec(5, mod_row),
                  pl.BlockSpec((1, 1, D_MODEL), lambda b, i: (l, 0, 0)),
                  pl.BlockSpec((1, D_MODEL, 2 * D_FF), lambda b, i: (l, 0, 0), **_SINGLE),
                  pl.BlockSpec((1, FFN_CONV, 2 * D_FF), lambda b, i: (l, 0, 0)),
                  pl.BlockSpec((1, D_FF, D_MODEL), lambda b, i: (l, 0, 0), **_SINGLE)],
        out_specs=pl.BlockSpec((1, tm, D_MODEL), lambda b, i: (b, i, 0)),
        out_shape=jax.ShapeDtypeStruct(xs.shape, F32),
        scratch_shapes=[pltpu.VMEM((tm + 2 * HALO, D_MODEL), F32),
                        pltpu.VMEM((tm + 2 * HALO, D_MODEL), BF16),
                        pltpu.VMEM((2, tm + 2 * HALO, FF_TILE), F32),
                        pltpu.VMEM((2, tm + 2 * HALO, FF_TILE), F32),
                        pltpu.VMEM((tm, D_FF), BF16)],
        compiler_params=_params(("parallel", "parallel")),
        name="convffn",
    )(xs, xs, xs, mod, mod, mod, norm_ffn, w_up, ffn_conv, w_down)


def _rope_tables():
    half = GDN_DH // 2
    nf = half // 2
    t = jnp.arange(SEQ)
    freqs = ROPE_THETA ** (-jnp.arange(nf, dtype=F32) / nf)

    def part(pos):
        ang = pos.astype(F32)[:, None] * freqs
        cos, sin = jnp.cos(ang), jnp.sin(ang)
        return jnp.concatenate([cos, cos], axis=-1), jnp.concatenate([-sin, sin], axis=-1)

    c_r, s_r = part(t // GRID_W)
    c_c, s_c = part(t % GRID_W)
    return jnp.concatenate([c_r, c_c], axis=-1), jnp.concatenate([s_r, s_c], axis=-1)


def _pad_lanes(a, n=LANES):
    return jnp.pad(a, [(0, 0)] * (a.ndim - 1) + [(0, n - a.shape[-1])])


def kernel(x, c, ctx, c_ctx, w_ada, b_ada, norm_mix, w_in, pool_w, pool_scale, gdn_conv, gdn_a_log,
           gdn_dt_bias, gdn_norm, na_q_norm, na_k_norm, na_rpb, w_out, norm_ffn, w_up, ffn_conv, w_down):
    B = x.shape[0]
    depth = w_ada.shape[0]

    rows = -(-(B + 1) // SUBLANES) * SUBLANES
    cvec = jnp.concatenate([c, c_ctx[None, :], jnp.zeros((rows - B - 1, D_MODEL), F32)], axis=0)
    w_in_b = w_in.astype(BF16)
    wts = (w_in_b[:, :, :POOL_END], w_in_b[:, :, POOL_END:GQKV_END], w_in_b[:, :, GQKV_END:GZ_END],
           _pad_lanes(w_in_b[:, :, GZ_END:GAB_END]), w_in_b[:, :, GAB_END:])
    eye_g = jnp.eye(len(POOL_WINDOWS), dtype=F32)
    pool_bd = jnp.einsum("lgcd,gh->lgchd", pool_w, eye_g).reshape(depth, POOL_W, POOL_W).astype(BF16)
    alog = _pad_lanes(gdn_a_log.reshape(depth, 1, 2 * GDN_H))
    dtb = _pad_lanes(gdn_dt_bias.reshape(depth, 1, 2 * GDN_H))
    rope_tables = _rope_tables()
    qn_t = jnp.tile(na_q_norm, (1, NA_H)).reshape(depth, 1, NA_W)
    kn_t = jnp.tile(na_k_norm, (1, NA_H)).reshape(depth, 1, NA_W)
    w_out_b = w_out.astype(BF16)
    w_up_b = w_up.astype(BF16)
    w_down_b = w_down.astype(BF16)
    norm_mix3 = norm_mix.reshape(depth, 1, D_MODEL)
    norm_ffn3 = norm_ffn.reshape(depth, 1, D_MODEL)
    pool_scale3 = pool_scale.reshape(depth, 1, POOL_W)
    gdn_norm3 = gdn_norm.reshape(depth, 1, GDN_DH)
    s_zero = jnp.zeros((B, N_CHAIN, GDN_DH, GDN_DH), F32)
    na_bias = _na_bias_table(na_rpb)

    mods = _ada_call(cvec, w_ada, b_ada).reshape(depth, rows, 1, 6 * D_MODEL)

    for l in range(depth):
        mod = mods[l]
        with_ctx_out = l < depth - 1
        pv, gqkv, gz, gab, naq, nak, nav = _inproj_call(l, x, mod, None, norm_mix3, wts, qn_t, kn_t, gdn_conv,
                                                        rope_tables)
        cpv, cgqkv, cgz, cgab, cnaq, cnak, cnav = _inproj_call(l, ctx, mod, B, norm_mix3, wts, qn_t,
                                                               kn_t, gdn_conv, None)
        cprep = _gdn_prep_call(l, cgqkv, cgab, alog, dtb)
        co_f, co_b, s_ctx = _gdn_scan_call(cprep, s_zero)
        prep = _gdn_prep_call(l, gqkv, gab, alog, dtb)
        o_f, o_b, _ = _gdn_scan_call(prep, s_ctx)
        ya = _pool_call(l, pv, pool_bd, pool_scale3)
        yc = _na_call(l, naq, nak, nav, cnak, cnav, na_bias)
        x = _outproj_call(l, x, mod, None, ya, o_f, o_b, gz, yc, gdn_norm3, w_out_b, l > 0)
        x = _ffn_call(l, x, mod, None, norm_ffn3, w_up_b, ffn_conv, w_down_b)
        if with_ctx_out:
            cya = _pool_call(l, cpv, pool_bd, pool_scale3)
            cyc = _ctx_attn_call(cnaq, cnak, cnav)
            ctx = _outproj_call(l, ctx, mod, B, cya, co_f, co_b, cgz, cyc, gdn_norm3, w_out_b, l > 0)
            ctx = _ffn_call(l, ctx, mod, B, norm_ffn3, w_up_b, ffn_conv, w_down_b)
    return x
```

```python
import functools

import numpy as np
import jax
import jax.numpy as jnp
from jax import lax
from jax.experimental import pallas as pl
from jax.experimental.pallas import tpu as pltpu

F32 = jnp.float32
BF16 = jnp.bfloat16

D_MODEL = 1024
SEQ = 4096
CTX = 256
GRID_W = 64
EPS = 1e-6

POOL_W = 256
POOL_WINDOWS = (2, 4, 8, 16)
POOL_GD = 64
GDN_W = 512
GDN_H = 4
GDN_DH = 128
CHUNK = 64
SHORT_CONV = 5
ROPE_THETA = 10000.0
NA_W = 256
NA_H = 4
NA_DH = 64
NA_WIN_ROWS = 8
NA_WIN_COLS = 16
D_FF = 2816
FFN_CONV = 3

POOL_END = POOL_W
GQKV_END = POOL_END + 3 * GDN_W
GZ_END = GQKV_END + GDN_W
GAB_END = GZ_END + 4 * GDN_H

LANES = 128
SUBLANES = 8
HALO = SUBLANES
TM_MM = 512
TM_GDN = 256
FF_TILE = 256
FF_UP_CHUNKS = 3
NEG = -1e30
VMEM_LIMIT = 56 * 1024 * 1024

_HI = lax.Precision.HIGHEST
_SINGLE = dict(pipeline_mode=pl.Buffered(1))


def _dot(a, b):
    return jnp.dot(a, b, preferred_element_type=F32)


def _silu(x):
    return x * jax.nn.sigmoid(x)


def _params(sem, vmem=VMEM_LIMIT):
    return pltpu.CompilerParams(dimension_semantics=sem, vmem_limit_bytes=vmem)


def _tile(rows):
    return min(rows, TM_MM)


def _mod_spec(chunk, row):
    return pl.BlockSpec((1, 1, D_MODEL), lambda b, i: (b if row is None else row, 0, chunk))


def _halo_specs(tm, rows, width, col=None):
    nb = tm // HALO
    last = rows // HALO - 1
    c = 0 if col is None else col
    prev = pl.BlockSpec((1, HALO, width), lambda b, i: (b, jnp.maximum(i * nb - 1, 0), c))
    nxt = pl.BlockSpec((1, HALO, width), lambda b, i: (b, jnp.minimum((i + 1) * nb, last), c))
    return prev, nxt


def _fill_halo_buf(buf, prev_ref, cur_ref, next_ref, i, n_tiles, tm):
    buf[0:HALO, :] = jnp.where(i > 0, prev_ref[0], 0.0)
    buf[HALO:HALO + tm, :] = cur_ref[0]
    buf[HALO + tm:2 * HALO + tm, :] = jnp.where(i < n_tiles - 1, next_ref[0], 0.0)


def _ada_kernel(c_ref, w_ref, b_ref, o_ref):
    sc = _silu(c_ref[...]).astype(BF16)
    o_ref[0] = _dot(sc, w_ref[0].astype(BF16)) + b_ref[0]


def _ada_call(cvec, w_ada, b_ada):
    depth = w_ada.shape[0]
    rows = cvec.shape[0]
    tn = 1536
    return pl.pallas_call(
        _ada_kernel,
        grid=(depth, 6 * D_MODEL // tn),
        in_specs=[
            pl.BlockSpec((rows, D_MODEL), lambda l, j: (0, 0)),
            pl.BlockSpec((1, D_MODEL, tn), lambda l, j: (l, 0, j)),
            pl.BlockSpec((1, 1, tn), lambda l, j: (l, 0, j)),
        ],
        out_specs=pl.BlockSpec((1, rows, tn), lambda l, j: (l, 0, j)),
        out_shape=jax.ShapeDtypeStruct((depth, rows, 6 * D_MODEL), F32),
        compiler_params=_params(("parallel", "parallel")),
        name="adaln",
    )(cvec, w_ada, b_ada.reshape(depth, 1, 6 * D_MODEL))


def _inproj_kernel(*refs, tm, n_tiles, rope):
    if rope:
        (x_ref, xp_ref, xn_ref, sh_ref, sc_ref, nw_ref, wp_ref, wq_ref, wz_ref, wab_ref, wna_ref,
         qn_ref, kn_ref, cw_ref, cos_ref, sin_ref,
         pool_o, gqkv_o, gz_o, gab_o, naq_o, nak_o, nav_o, xbuf) = refs
    else:
        (x_ref, xp_ref, xn_ref, sh_ref, sc_ref, nw_ref, wp_ref, wq_ref, wz_ref, wab_ref, wna_ref,
         qn_ref, kn_ref, cw_ref,
         pool_o, gqkv_o, gz_o, gab_o, naq_o, nak_o, nav_o, xbuf) = refs
    i = pl.program_id(1)
    _fill_halo_buf(xbuf, xp_ref, x_ref, xn_ref, i, n_tiles, tm)
    x = xbuf[...]
    r = lax.rsqrt(jnp.mean(x * x, axis=-1, keepdims=True) + EPS)
    h = (x * r * nw_ref[0]) * (1.0 + sc_ref[0]) + sh_ref[0]
    rows = lax.broadcasted_iota(jnp.int32, (tm + 2 * HALO, 1), 0)
    ok = jnp.logical_and(jnp.logical_or(rows >= HALO, i > 0),
                         jnp.logical_or(rows < HALO + tm, i < n_tiles - 1))
    hb_full = jnp.where(ok, h, 0.0).astype(BF16)
    hb = hb_full[HALO:HALO + tm]

    p_qkv = _dot(hb_full, wq_ref[0])
    nb = tm + 2 * HALO
    acc = None
    for k in range(SHORT_CONV):
        shift = (SHORT_CONV // 2 - k) % nb
        xs = p_qkv if shift == 0 else pltpu.roll(p_qkv, shift, axis=0)
        term = xs[HALO:HALO + tm] * cw_ref[0, k:k + 1, :]
        acc = term if acc is None else acc + term
    qkv = _silu(acc)
    lane = lax.broadcasted_iota(jnp.int32, (tm, GDN_DH), 1)
    first = (lane % (GDN_DH // 2)) < (GDN_DH // 4)

    def norm_rope(xv):
        xn = xv * lax.rsqrt(jnp.sum(xv * xv, axis=-1, keepdims=True) + EPS)
        if not rope:
            return xn
        partner = jnp.where(first, pltpu.roll(xn, GDN_DH - GDN_DH // 4, axis=1),
                            pltpu.roll(xn, GDN_DH // 4, axis=1))
        return xn * cos_ref[...] + partner * sin_ref[...]

    for hd in range(GDN_H):
        cq = slice(hd * GDN_DH, (hd + 1) * GDN_DH)
        ck = slice(GDN_W + hd * GDN_DH, GDN_W + (hd + 1) * GDN_DH)
        gqkv_o[0, :, cq] = norm_rope(qkv[:, cq]) * (GDN_DH ** -0.5)
        gqkv_o[0, :, ck] = norm_rope(qkv[:, ck])
    gqkv_o[0, :, 2 * GDN_W:] = qkv[:, 2 * GDN_W:]

    pool_o[0] = _dot(hb, wp_ref[0])
    gz_o[0] = _dot(hb, wz_ref[0])
    gab_o[0] = _dot(hb, wab_ref[0])
    na = _dot(hb, wna_ref[0])
    li = lax.broadcasted_iota(jnp.int32, (NA_W, NA_W), 0) // NA_DH
    lj = lax.broadcasted_iota(jnp.int32, (NA_W, NA_W), 1) // NA_DH
    seg = (li == lj).astype(BF16)
    q = na[:, 0:NA_W]
    k = na[:, NA_W:2 * NA_W]

    def seg_mean(sq):
        hi = sq.astype(BF16)
        lo = (sq - hi.astype(F32)).astype(BF16)
        return (_dot(hi, seg) + _dot(lo, seg)) * (1.0 / NA_DH)

    qs = seg_mean(q * q)
    ks = seg_mean(k * k)
    qn = q * lax.rsqrt(qs + EPS) * qn_ref[0] * (NA_DH ** -0.5)
    kn = k * lax.rsqrt(ks + EPS) * kn_ref[0]
    v = na[:, 2 * NA_W:3 * NA_W]

    def with_swizzle(t):
        return jnp.concatenate([t, pltpu.roll(t, NA_DH, axis=1)], axis=1).astype(BF16)

    naq_o[0] = with_swizzle(qn)
    nak_o[0] = with_swizzle(kn)
    nav_o[0] = with_swizzle(v)


def _inproj_call(l, xs, mod, mod_row, norm_mix, wts, qn_t, kn_t, conv_w, rope_tables):
    B, rows, _ = xs.shape
    tm = _tile(rows)
    n = rows // tm
    rope = rope_tables is not None
    prev, nxt = _halo_specs(tm, rows, D_MODEL)

    def wspec(w):
        return pl.BlockSpec((1,) + w.shape[1:], lambda b, i: (l, 0, 0), **_SINGLE)

    def rowspec(width):
        return pl.BlockSpec((1, tm, width), lambda b, i: (b, i, 0))

    def vec(width, sub=1):
        return pl.BlockSpec((1, sub, width), lambda b, i: (l, 0, 0))

    in_specs = ([rowspec(D_MODEL), prev, nxt, _mod_spec(0, mod_row), _mod_spec(1, mod_row), vec(D_MODEL)]
                + [wspec(w) for w in wts] + [vec(NA_W), vec(NA_W), vec(3 * GDN_W, SHORT_CONV)])
    args = [xs, xs, xs, mod, mod, norm_mix, *wts, qn_t, kn_t, conv_w]
    if rope:
        in_specs += [pl.BlockSpec((tm, GDN_DH), lambda b, i: (i, 0))] * 2
        args += list(rope_tables)
    outs = [(POOL_W, F32), (3 * GDN_W, F32), (GDN_W, F32), (LANES, F32),
            (2 * NA_W, BF16), (2 * NA_W, BF16), (2 * NA_W, BF16)]
    return pl.pallas_call(
        functools.partial(_inproj_kernel, tm=tm, n_tiles=n, rope=rope),
        grid=(B, n),
        in_specs=in_specs,
        out_specs=[rowspec(width) for width, _ in outs],
        out_shape=[jax.ShapeDtypeStruct((B, rows, width), dt) for width, dt in outs],
        scratch_shapes=[pltpu.VMEM((tm + 2 * HALO, D_MODEL), F32)],
        compiler_params=_params(("parallel", "parallel")),
        name="inproj",
    )(*args)


CPT = TM_GDN // CHUNK
N_CHAIN = 2 * GDN_H
SCAN_NB = 2


def _bdot(a, b):
    return lax.dot_general(a.astype(BF16), b.astype(BF16), (((2,), (1,)), ((0,), (0,))),
                           preferred_element_type=F32)


def _bdot_nt(a, b):
    return lax.dot_general(a.astype(BF16), b.astype(BF16), (((2,), (2,)), ((0,), (0,))),
                           preferred_element_type=F32)


def _split3(x):
    hi = x.astype(BF16)
    r1 = x - hi.astype(F32)
    mid = r1.astype(BF16)
    lo = (r1 - mid.astype(F32)).astype(BF16)
    return hi, mid, lo


def _gdn_prep_kernel(qkv_ref, gab_ref, alog_ref, dtb_ref,
                     uf_o, ub_o, wqf_o, wqb_o, ikf_o, ikb_o, gl_o, gt_buf):
    tm = TM_GDN
    qkv = qkv_ref[0]

    gab = gab_ref[0]
    g_all = -jnp.exp(alog_ref[0]) * jax.nn.softplus(gab + dtb_ref[0])
    beta_all = jax.nn.sigmoid(gab)
    ri = lax.broadcasted_iota(jnp.int32, (tm, tm), 0)
    rj = lax.broadcasted_iota(jnp.int32, (tm, tm), 1)
    same = (ri // CHUNK) == (rj // CHUNK)
    low = jnp.logical_and(same, rj <= ri).astype(BF16)
    upp = jnp.logical_and(same, rj >= ri).astype(BF16)
    g_hi, g_mid, g_lo = _split3(g_all)
    pre = _dot(low, g_hi) + _dot(low, g_mid) + _dot(low, g_lo)
    suf = _dot(upp, g_hi) + _dot(upp, g_mid) + _dot(upp, g_lo)
    glane = lax.broadcasted_iota(jnp.int32, (tm, LANES), 1)
    gc_all = jnp.where(glane < GDN_H, pre, suf)
    gt_buf[...] = gc_all.T

    pi = lax.broadcasted_iota(jnp.int32, (CHUNK, 2 * CHUNK), 0)
    pj = lax.broadcasted_iota(jnp.int32, (CHUNK, 2 * CHUNK), 1)
    is_fwd = pj < CHUNK
    pjj = jnp.where(is_fwd, pj, pj - CHUNK)
    sgn = jnp.where(is_fwd, 1, -1)
    incl = (pi - pjj) * sgn >= 0
    strict = (pi - pjj) * sgn > 0

    fwd_half = jnp.where(is_fwd, 1.0, 0.0).astype(BF16)
    bwd_half = jnp.where(is_fwd, 0.0, 1.0).astype(BF16)

    def block_diag(x):
        xb16 = x.astype(BF16)
        return jnp.concatenate([xb16 * fwd_half, xb16 * bwd_half], axis=1)

    lhs_l, k2_l, decay_l, beta_l, rhs_l, qd_l, kd_l, gl_l = [], [], [], [], [], [], [], []
    for h in range(GDN_H):
        q3 = qkv[:, h * GDN_DH:(h + 1) * GDN_DH].reshape(CPT, CHUNK, GDN_DH)
        k3 = qkv[:, GDN_W + h * GDN_DH:GDN_W + (h + 1) * GDN_DH].reshape(CPT, CHUNK, GDN_DH)
        v3 = qkv[:, 2 * GDN_W + h * GDN_DH:2 * GDN_W + (h + 1) * GDN_DH].reshape(CPT, CHUNK, GDN_DH)
        kb16 = k3.astype(BF16)
        lhs_l.append(jnp.concatenate([kb16, q3.astype(BF16)], axis=1))
        k2_l.append(jnp.concatenate([kb16, kb16], axis=1))
        gcols, betas, grows, rhs_d = [], [], [], []
        for d in range(2):
            gi = d * GDN_H + h
            bi = 2 * GDN_H + gi
            gcol = jnp.broadcast_to(gc_all[:, gi:gi + 1], (tm, GDN_DH)).reshape(CPT, CHUNK, GDN_DH)
            beta = jnp.broadcast_to(beta_all[:, bi:bi + 1], (tm, GDN_DH)).reshape(CPT, CHUNK, GDN_DH)
            grow_full = gt_buf[gi:gi + 1, :]
            grows.append([grow_full[:, c * CHUNK:(c + 1) * CHUNK] for c in range(CPT)])
            e = jnp.exp(gcol)
            rhs_d.append(jnp.concatenate([v3 * beta, k3 * (beta * e)], axis=2))
            g_last = gcol[:, 0:1, :] if d else gcol[:, CHUNK - 1:CHUNK, :]
            qd_l.append(q3 * e)
            kd_l.append(k3 * jnp.exp(g_last - gcol))
            gl_l.append(jnp.exp(g_last))
            gcols.append(gcol)
            betas.append(beta)
        grow = jnp.stack([jnp.concatenate([grows[0][c], grows[1][c]], axis=1) for c in range(CPT)],
                         axis=0)
        diff = jnp.where(is_fwd, gcols[0], gcols[1]) - grow
        decay_l.append(jnp.where(incl, jnp.exp(jnp.where(incl, diff, 0.0)), 0.0))
        beta_l.append(jnp.where(is_fwd, betas[0], betas[1]))
        rhs_l.append(jnp.concatenate(rhs_d, axis=1))

    decay = jnp.concatenate(decay_l, axis=0)
    beta_p = jnp.concatenate(beta_l, axis=0)
    m1 = _bdot_nt(jnp.concatenate(lhs_l, axis=0), jnp.concatenate(k2_l, axis=0))
    a_mat = jnp.where(strict, m1[:, :CHUNK] * (decay * beta_p), 0.0)
    intra = m1[:, CHUNK:] * decay
    p = -a_mat
    t_off = p
    for _ in range(5):
        p = _bdot(p, block_diag(p))
        t_off = t_off + p + _bdot(t_off, block_diag(p))
    rhs = jnp.concatenate(rhs_l, axis=0)
    uw = rhs + _bdot(block_diag(t_off), rhs)

    outs = ((uf_o, wqf_o, ikf_o), (ub_o, wqb_o, ikb_o))
    for d in range(2):
        u_o, wq_o, ik_o = outs[d]
        prow = slice(d * CHUNK, (d + 1) * CHUNK)
        for h in range(GDN_H):
            n0 = h * CPT
            cols = slice(h * GDN_DH, (h + 1) * GDN_DH)
            u_o[0, :, cols] = uw[n0:n0 + CPT, prow, :GDN_DH].reshape(tm, GDN_DH).astype(BF16)
            wq_o[0, h, :, 0:CHUNK, :] = uw[n0:n0 + CPT, prow, GDN_DH:].astype(BF16)
            wq_o[0, h, :, CHUNK:, :] = qd_l[2 * h + d].astype(BF16)
            ik_o[0, h, :, 0:CHUNK, :] = intra[n0:n0 + CPT, :, prow].astype(BF16)
            kd3 = kd_l[2 * h + d]
            for c in range(CPT):
                ik_o[0, h, c, CHUNK:, :] = kd3[c].T.astype(BF16)
            gl_o[0, h, :, d:d + 1, :] = gl_l[2 * h + d]


def _gdn_prep_call(l, qkv, gab, alog, dtb):
    B, rows, _ = qkv.shape
    tm = TM_GDN
    n = rows // tm

    def vec(width):
        return pl.BlockSpec((1, 1, width), lambda b, i: (l, 0, 0))

    in_specs = [pl.BlockSpec((1, tm, 3 * GDN_W), lambda b, i: (b, i, 0)),
                pl.BlockSpec((1, tm, LANES), lambda b, i: (b, i, 0)), vec(LANES), vec(LANES)]
    args = [qkv, gab, alog, dtb]
    nc = rows // CHUNK
    u_spec = pl.BlockSpec((1, tm, GDN_W), lambda b, i: (b, i, 0))
    wq_spec = pl.BlockSpec((1, GDN_H, CPT, 2 * CHUNK, GDN_DH), lambda b, i: (b, 0, i, 0, 0))
    ik_spec = pl.BlockSpec((1, GDN_H, CPT, 3 * CHUNK, CHUNK), lambda b, i: (b, 0, i, 0, 0))
    gl_spec = pl.BlockSpec((1, GDN_H, CPT, 2, LANES), lambda b, i: (b, 0, i, 0, 0))
    u_sh = jax.ShapeDtypeStruct((B, rows, GDN_W), BF16)
    wq_sh = jax.ShapeDtypeStruct((B, GDN_H, nc, 2 * CHUNK, GDN_DH), BF16)
    ik_sh = jax.ShapeDtypeStruct((B, GDN_H, nc, 3 * CHUNK, CHUNK), BF16)
    gl_sh = jax.ShapeDtypeStruct((B, GDN_H, nc, 2, LANES), F32)
    return pl.pallas_call(
        _gdn_prep_kernel,
        grid=(B, n),
        in_specs=in_specs,
        out_specs=[u_spec, u_spec, wq_spec, wq_spec, ik_spec, ik_spec, gl_spec],
        out_shape=[u_sh, u_sh, wq_sh, wq_sh, ik_sh, ik_sh, gl_sh],
        scratch_shapes=[pltpu.VMEM((LANES, tm), F32)],
        compiler_params=_params(("parallel", "parallel")),
        name="gdn_prep",
    )(*args)


def _gdn_scan_kernel(uf_ref, ub_ref, wqf_ref, wqb_ref, ikf_ref, ikb_ref, glf_ref, glb_ref, s0_ref,
                     of_o, ob_o, s_o, s_ref, *, n_tiles, nb):
    i = pl.program_id(1)

    @pl.when(i == 0)
    def _():
        s_ref[...] = s0_ref[...]

    refs = ((uf_ref, wqf_ref, ikf_ref, glf_ref, of_o), (ub_ref, wqb_ref, ikb_ref, glb_ref, ob_o))
    chains = [(e, d, h) for e in range(nb) for d in range(2) for h in range(GDN_H)]
    states = [s_ref[e, d * GDN_H + h] for e, d, h in chains]
    for step in range(CPT):
        ws_qs = []
        for n, (e, d, h) in enumerate(chains):
            c = CPT - 1 - step if d else step
            ws_qs.append(_dot(refs[d][1][e, h, c], states[n].astype(BF16)))
        vb = []
        for n, (e, d, h) in enumerate(chains):
            c = CPT - 1 - step if d else step
            u = refs[d][0][e, c * CHUNK:(c + 1) * CHUNK, h * GDN_DH:(h + 1) * GDN_DH]
            vb.append((u - ws_qs[n][:CHUNK]).astype(BF16))
        r = []
        for n, (e, d, h) in enumerate(chains):
            c = CPT - 1 - step if d else step
            r.append(_dot(refs[d][2][e, h, c], vb[n]))
        for n, (e, d, h) in enumerate(chains):
            c = CPT - 1 - step if d else step
            refs[d][4][e, c * CHUNK:(c + 1) * CHUNK, h * GDN_DH:(h + 1) * GDN_DH] = (
                ws_qs[n][CHUNK:] + r[n][:CHUNK]).astype(BF16)
            states[n] = states[n] * refs[d][3][e, h, c, d:d + 1, :] + r[n][CHUNK:]
    for n, (e, d, h) in enumerate(chains):
        s_ref[e, d * GDN_H + h] = states[n]

    @pl.when(i == n_tiles - 1)
    def _():
        s_o[...] = s_ref[...]


def _gdn_scan_call(prep, s0):
    uf, ub, wqf, wqb, ikf, ikb, gl = prep
    B, rows, _ = uf.shape
    tm = TM_GDN
    n = rows // tm
    nb = SCAN_NB if B % SCAN_NB == 0 else 1

    def ident(i):
        return i

    def rev(i):
        return n - 1 - i

    def rowspec(order):
        return pl.BlockSpec((nb, tm, GDN_W), lambda b, i: (b, order(i), 0))

    def tilespec(arr, order):
        return pl.BlockSpec((nb, GDN_H, CPT) + arr.shape[3:], lambda b, i: (b, 0, order(i), 0, 0))

    s_spec = pl.BlockSpec((nb, N_CHAIN, GDN_DH, GDN_DH), lambda b, i: (b, 0, 0, 0))
    o_shape = jax.ShapeDtypeStruct((B, rows, GDN_W), BF16)
    return pl.pallas_call(
        functools.partial(_gdn_scan_kernel, n_tiles=n, nb=nb),
        grid=(B // nb, n),
        in_specs=[rowspec(ident), rowspec(rev), tilespec(wqf, ident), tilespec(wqb, rev),
                  tilespec(ikf, ident), tilespec(ikb, rev), tilespec(gl, ident), tilespec(gl, rev),
                  s_spec],
        out_specs=[rowspec(ident), rowspec(rev), s_spec],
        out_shape=[o_shape, o_shape, jax.ShapeDtypeStruct(s0.shape, F32)],
        scratch_shapes=[pltpu.VMEM((nb, N_CHAIN, GDN_DH, GDN_DH), F32)],
        compiler_params=_params(("parallel", "arbitrary")),
        name="gdn_scan",
    )(uf, ub, wqf, wqb, ikf, ikb, gl, gl, s0)


NA_ROWS = SEQ // GRID_W
NA_BAND = NA_WIN_ROWS * GRID_W
NA_RPS = 8
_NT = (((1,), (1,)), ((), ()))


def _na_head_cols(h):
    off = (h // 2) * LANES if h % 2 == 0 else NA_W + ((h // 2 + 1) % (NA_H // 2)) * LANES
    return slice(off, off + NA_DH)


def _lane_fold(parts, op):
    tiles = [p[:, t * LANES:(t + 1) * LANES] for p in parts for t in range(p.shape[1] // LANES)]
    acc = tiles[0]
    for t in tiles[1:]:
        acc = op(acc, t)
    return acc


def _na_kernel(q_ref, k_ref, v_ref, kc_ref, vc_ref, bias_ref, o_ref):
    step = pl.program_id(1)
    q = q_ref[0]
    kc = kc_ref[0]
    vc = vc_ref[0]
    heads = [_na_head_cols(h) for h in range(NA_H)]
    qrows = [slice(j * GRID_W, (j + 1) * GRID_W) for j in range(NA_RPS)]
    kb, vb, d0 = [], [], []
    for j in range(NA_RPS):
        r = step * NA_RPS + j
        r0 = jnp.clip(r - NA_WIN_ROWS // 2, 0, NA_ROWS - NA_WIN_ROWS)
        d0.append(r0 - r + NA_WIN_ROWS - 1)
        start = pl.multiple_of(r0 * GRID_W, GRID_W)
        kb.append(k_ref[0, pl.ds(start, NA_BAND), :])
        vb.append(v_ref[0, pl.ds(start, NA_BAND), :])

    def scores(j):
        return [lax.dot_general(q[qrows[j], c], kb[j][:, c], _NT, preferred_element_type=F32)
                for c in heads]

    s_ctx = [lax.dot_general(q[:, c], kc[:, c], _NT, preferred_element_type=F32) for c in heads]
    p_ctx = [[None] * NA_RPS for _ in range(NA_H)]
    den = [[None] * NA_H for _ in range(NA_RPS)]
    o_loc = [None] * NA_RPS
    s_next = scores(0)
    for j in range(NA_RPS):
        s_loc = s_next
        if j + 1 < NA_RPS:
            s_next = scores(j + 1)
        p_loc = []
        for h in range(NA_H):
            sl = s_loc[h] + bias_ref[h, d0[j]]
            sc = s_ctx[h][qrows[j]]
            m = jnp.max(_lane_fold([sl, sc], jnp.maximum), axis=-1, keepdims=True)
            pl_ = jnp.exp(sl - m)
            pc_ = jnp.exp(sc - m)
            den[j][h] = jnp.sum(_lane_fold([pl_, pc_], jnp.add), axis=-1, keepdims=True)
            p_loc.append(pl_.astype(BF16))
            p_ctx[h][j] = pc_.astype(BF16)
        o_loc[j] = [_dot(p_loc[h], vb[j][:, c]) for h, c in enumerate(heads)]
    o_ctx = [_dot(jnp.concatenate(p_ctx[h], axis=0), vc[:, c]) for h, c in enumerate(heads)]
    for j in range(NA_RPS):
        o_ref[0, qrows[j], :] = jnp.concatenate(
            [(o_loc[j][h] + o_ctx[h][qrows[j]]) / den[j][h] for h in range(NA_H)], axis=-1).astype(BF16)


def _na_call(l, q, k, v, kc, vc, bias):
    B = q.shape[0]
    full = pl.BlockSpec((1, SEQ, 2 * NA_W), lambda b, r: (b, 0, 0))
    cfull = pl.BlockSpec((1, CTX, 2 * NA_W), lambda b, r: (b, 0, 0))
    blk = pl.BlockSpec((1, NA_RPS * GRID_W, 2 * NA_W), lambda b, r: (b, r, 0))
    oblk = pl.BlockSpec((1, NA_RPS * GRID_W, NA_W), lambda b, r: (b, r, 0))
    return pl.pallas_call(
        _na_kernel,
        grid=(B, NA_ROWS // NA_RPS),
        in_specs=[blk, full, full, cfull, cfull,
                  pl.BlockSpec((NA_H,) + bias.shape[1:], lambda b, r: (l, 0, 0, 0))],
        out_specs=oblk,
        out_shape=jax.ShapeDtypeStruct((B, SEQ, NA_W), BF16),
        compiler_params=_params(("parallel", "arbitrary")),
        name="natten",
    )(q, k, v, kc, vc, bias)


def _ctx_attn_kernel(q_ref, k_ref, v_ref, o_ref):
    q, k, v = q_ref[0], k_ref[0], v_ref[0]
    outs = []
    for h in range(NA_H):
        cols = _na_head_cols(h)
        s = lax.dot_general(q[:, cols], k[:, cols], _NT, preferred_element_type=F32)
        p = jnp.exp(s - jnp.max(s, axis=-1, keepdims=True))
        den = jnp.sum(p, axis=-1, keepdims=True)
        outs.append(_dot(p.astype(BF16), v[:, cols]) / den)
    o_ref[0] = jnp.concatenate(outs, axis=-1).astype(BF16)


def _ctx_attn_call(q, k, v):
    B = q.shape[0]
    spec = pl.BlockSpec((1, CTX, 2 * NA_W), lambda b: (b, 0, 0))
    ospec = pl.BlockSpec((1, CTX, NA_W), lambda b: (b, 0, 0))
    return pl.pallas_call(
        _ctx_attn_kernel,
        grid=(B,),
        in_specs=[spec, spec, spec],
        out_specs=ospec,
        out_shape=jax.ShapeDtypeStruct((B, CTX, NA_W), BF16),
        compiler_params=_params(("parallel",)),
        name="ctx_attn",
    )(q, k, v)


def _na_bias_table(rpb):
    depth = rpb.shape[0]
    qc = np.arange(GRID_W)[:, None]
    kc = np.arange(GRID_W)[None, :]
    c0 = np.clip(qc - NA_WIN_COLS // 2, 0, GRID_W - NA_WIN_COLS)
    mask = (kc >= c0) & (kc < c0 + NA_WIN_COLS)
    dc = kc - qc + NA_WIN_COLS - 1
    n_dr, n_dc = 2 * NA_WIN_ROWS - 1, 2 * NA_WIN_COLS - 1
    sel_c = (np.arange(n_dc)[:, None, None] == dc[None]) & mask[None]
    dr = np.arange(NA_WIN_ROWS)[:, None] + np.arange(NA_WIN_ROWS)[None, :]
    sel_r = np.arange(n_dr)[:, None, None] == dr[None]
    t = jnp.einsum("lhrc,rdk,cqn->lhdqkn", rpb, jnp.asarray(sel_r, F32), jnp.asarray(sel_c, F32),
                   precision=_HI)
    t = jnp.where(mask[:, None, :], t, NEG)
    return t.reshape(depth * NA_H, NA_WIN_ROWS, GRID_W, NA_BAND)


def _outproj_kernel(x_ref, pv_ref, pvp_ref, pvn_ref, of_ref, ob_ref, z_ref, yc_ref, g_ref, gn_ref,
                    pw_ref, ps_ref, w_ref, o_ref, buf, *, tm, rows):
    i = pl.program_id(1)
    _fill_halo_buf(buf, pvp_ref, pv_ref, pvn_ref, i, rows // tm, tm)
    nb = tm + 2 * HALO
    xb = buf[...]

    def shifted(a, k):
        return pltpu.roll(a, k % nb, axis=0)

    a2 = xb + shifted(xb, 1)
    a4 = shifted(a2, 1) + shifted(a2, -1)
    a8 = shifted(a4, 2) + shifted(a4, -2)
    a16 = shifted(a8, 4) + shifted(a8, -4)
    t = lax.broadcasted_iota(jnp.int32, (tm, POOL_W), 0) + i * tm
    grp = lax.broadcasted_iota(jnp.int32, (tm, POOL_W), 1) // POOL_GD
    half = jnp.left_shift(1, grp)
    cnt = (jnp.minimum(t + half, rows) - jnp.maximum(t - half, 0)).astype(F32)
    inner = slice(HALO, HALO + tm)
    pooled = jnp.where(grp == 0, a2[inner],
                       jnp.where(grp == 1, a4[inner], jnp.where(grp == 2, a8[inner], a16[inner])))
    pooled = pooled / cnt - pv_ref[0]
    ya = (_dot(pooled.astype(BF16), pw_ref[0]) * ps_ref[0]).astype(BF16)

    acc = _dot(ya, w_ref[0, 0:POOL_W, :])
    o = of_ref[0].astype(F32) + ob_ref[0].astype(F32)
    z = z_ref[0]
    for h in range(GDN_H):
        cols = slice(h * GDN_DH, (h + 1) * GDN_DH)
        oh = o[:, cols]
        on = oh * lax.rsqrt(jnp.mean(oh * oh, axis=-1, keepdims=True) + EPS) * gn_ref[0]
        yb = (on * _silu(z[:, cols])).astype(BF16)
        acc = acc + _dot(yb, w_ref[0, POOL_W + h * GDN_DH:POOL_W + (h + 1) * GDN_DH, :])
    acc = acc + _dot(yc_ref[0], w_ref[0, POOL_W + GDN_W:, :])
    o_ref[0] = x_ref[0] + g_ref[0] * acc


def _outproj_call(l, xs, mod, mod_row, pv, o_f, o_b, gz, yc, gdn_norm, pool_bd, pool_scale, w_out, in_place):
    B, rows, _ = xs.shape
    tm = _tile(rows)
    prev, nxt = _halo_specs(tm, rows, POOL_W)

    def rowspec(n):
        return pl.BlockSpec((1, tm, n), lambda b, i: (b, i, 0))

    def vec(width, sub=1):
        return pl.BlockSpec((1, sub, width), lambda b, i: (l, 0, 0))

    return pl.pallas_call(
        functools.partial(_outproj_kernel, tm=tm, rows=rows),
        grid=(B, rows // tm),
        in_specs=[rowspec(D_MODEL), rowspec(POOL_W), prev, nxt, rowspec(GDN_W), rowspec(GDN_W),
                  rowspec(GDN_W), rowspec(NA_W), _mod_spec(2, mod_row), vec(GDN_DH),
                  vec(POOL_W, POOL_W), vec(POOL_W),
                  pl.BlockSpec((1, D_MODEL, D_MODEL), lambda b, i: (l, 0, 0), **_SINGLE)],
        out_specs=rowspec(D_MODEL),
        out_shape=jax.ShapeDtypeStruct(xs.shape, F32),
        scratch_shapes=[pltpu.VMEM((tm + 2 * HALO, POOL_W), F32)],
        input_output_aliases={0: 0} if in_place else {},
        compiler_params=_params(("parallel", "parallel")),
        name="outproj",
    )(xs, pv, pv, pv, o_f, o_b, gz, yc, mod, gdn_norm, pool_bd, pool_scale, w_out)


def _ffn_kernel(cur_ref, prev_ref, next_ref, sh_ref, sc_ref, g_ref, nw_ref, wu_ref, cw_ref, wd_ref,
                o_ref, xbuf, hbuf, ua_buf, ub_buf, act_buf, *, tm, n_tiles):
    i = pl.program_id(1)
    _fill_halo_buf(xbuf, prev_ref, cur_ref, next_ref, i, n_tiles, tm)
    x = xbuf[...]
    r = lax.rsqrt(jnp.mean(x * x, axis=-1, keepdims=True) + EPS)
    h = (x * r * nw_ref[0]) * (1.0 + sc_ref[0]) + sh_ref[0]
    rows = lax.broadcasted_iota(jnp.int32, (tm + 2 * HALO, 1), 0)
    ok = jnp.logical_and(jnp.logical_or(rows >= HALO, i > 0),
                         jnp.logical_or(rows < HALO + tm, i < n_tiles - 1))
    hbuf[...] = jnp.where(ok, h, 0.0).astype(BF16)
    n_ff = D_FF // FF_TILE
    n_up = FF_UP_CHUNKS if (tm + 2 * HALO) % (FF_UP_CHUNKS * 2 * SUBLANES) == 0 else 1
    up_rows = (tm + 2 * HALO) // n_up

    def up(j):
        wa = wu_ref[0, :, j * FF_TILE:(j + 1) * FF_TILE]
        wb = wu_ref[0, :, D_FF + j * FF_TILE:D_FF + (j + 1) * FF_TILE]
        for m in range(n_up):
            rs = slice(m * up_rows, (m + 1) * up_rows)
            ua_buf[j % 2, rs, :] = _dot(hbuf[rs, :], wa)
            ub_buf[j % 2, rs, :] = _dot(hbuf[rs, :], wb)

    up(0)
    for j in range(n_ff):
        if j + 1 < n_ff:
            up(j + 1)
        ca = slice(j * FF_TILE, (j + 1) * FF_TILE)
        cb = slice(D_FF + j * FF_TILE, D_FF + (j + 1) * FF_TILE)
        a = ua_buf[j % 2, pl.ds(HALO - 1, tm), :] * cw_ref[0, 0:1, ca]
        b = ub_buf[j % 2, pl.ds(HALO - 1, tm), :] * cw_ref[0, 0:1, cb]
        for k in range(1, FFN_CONV):
            off = HALO + k - FFN_CONV // 2
            a = a + ua_buf[j % 2, pl.ds(off, tm), :] * cw_ref[0, k:k + 1, ca]
            b = b + ub_buf[j % 2, pl.ds(off, tm), :] * cw_ref[0, k:k + 1, cb]
        act_buf[:, ca] = (_silu(a) * b).astype(BF16)
    for m in range(2):
        rs = slice(m * (tm // 2), (m + 1) * (tm // 2))
        o_ref[0, rs, :] = cur_ref[0, rs, :] + g_ref[0] * _dot(act_buf[rs, :], wd_ref[0])


def _ffn_call(l, xs, mod, mod_row, norm_ffn, w_up, ffn_conv, w_down):
    B, rows, _ = xs.shape
    tm = _tile(rows)
    n = rows // tm
    prev, nxt = _halo_specs(tm, rows, D_MODEL)
    return pl.pallas_call(
        functools.partial(_ffn_kernel, tm=tm, n_tiles=n),
        grid=(B, n),
        in_specs=[pl.BlockSpec((1, tm, D_MODEL), lambda b, i: (b, i, 0)), prev, nxt,
                  _mod_spec(3, mod_row), _mod_spec(4, mod_row), _mod_spec(5, mod_row),
                  pl.BlockSpec((1, 1, D_MODEL), lambda b, i: (l, 0, 0)),
                  pl.BlockSpec((1, D_MODEL, 2 * D_FF), lambda b, i: (l, 0, 0), **_SINGLE),
                  pl.BlockSpec((1, FFN_CONV, 2 * D_FF), lambda b, i: (l, 0, 0)),
                  pl.BlockSpec((1, D_FF, D_MODEL), lambda b, i: (l, 0, 0), **_SINGLE)],
        out_specs=pl.BlockSpec((1, tm, D_MODEL), lambda b, i: (b, i, 0)),
        out_shape=jax.ShapeDtypeStruct(xs.shape, F32),
        scratch_shapes=[pltpu.VMEM((tm + 2 * HALO, D_MODEL), F32),
                        pltpu.VMEM((tm + 2 * HALO, D_MODEL), BF16),
                        pltpu.VMEM((2, tm + 2 * HALO, FF_TILE), F32),
                        pltpu.VMEM((2, tm + 2 * HALO, FF_TILE), F32),
                        pltpu.VMEM((tm, D_FF), BF16)],
        compiler_params=_params(("parallel", "parallel")),
        name="convffn",
    )(xs, xs, xs, mod, mod, mod, norm_ffn, w_up, ffn_conv, w_down)


def _rope_tables():
    half = GDN_DH // 2
    nf = half // 2
    t = jnp.arange(SEQ)
    freqs = ROPE_THETA ** (-jnp.arange(nf, dtype=F32) / nf)

    def part(pos):
        ang = pos.astype(F32)[:, None] * freqs
        cos, sin = jnp.cos(ang), jnp.sin(ang)
        return jnp.concatenate([cos, cos], axis=-1), jnp.concatenate([-sin, sin], axis=-1)

    c_r, s_r = part(t // GRID_W)
    c_c, s_c = part(t % GRID_W)
    return jnp.concatenate([c_r, c_c], axis=-1), jnp.concatenate([s_r, s_c], axis=-1)


def _pad_lanes(a, n=LANES):
    return jnp.pad(a, [(0, 0)] * (a.ndim - 1) + [(0, n - a.shape[-1])])


def kernel(x, c, ctx, c_ctx, w_ada, b_ada, norm_mix, w_in, pool_w, pool_scale, gdn_conv, gdn_a_log,
           gdn_dt_bias, gdn_norm, na_q_norm, na_k_norm, na_rpb, w_out, norm_ffn, w_up, ffn_conv, w_down):
    B = x.shape[0]
    depth = w_ada.shape[0]

    rows = -(-(B + 1) // SUBLANES) * SUBLANES
    cvec = jnp.concatenate([c, c_ctx[None, :], jnp.zeros((rows - B - 1, D_MODEL), F32)], axis=0)
    w_in_b = w_in.astype(BF16)
    wts = (w_in_b[:, :, :POOL_END], w_in_b[:, :, POOL_END:GQKV_END], w_in_b[:, :, GQKV_END:GZ_END],
           _pad_lanes(w_in_b[:, :, GZ_END:GAB_END]), w_in_b[:, :, GAB_END:])
    eye_g = jnp.eye(len(POOL_WINDOWS), dtype=F32)
    pool_bd = jnp.einsum("lgcd,gh->lgchd", pool_w, eye_g).reshape(depth, POOL_W, POOL_W).astype(BF16)
    alog = _pad_lanes(gdn_a_log.reshape(depth, 1, 2 * GDN_H))
    dtb = _pad_lanes(gdn_dt_bias.reshape(depth, 1, 2 * GDN_H))
    rope_tables = _rope_tables()
    qn_t = jnp.tile(na_q_norm, (1, NA_H)).reshape(depth, 1, NA_W)
    kn_t = jnp.tile(na_k_norm, (1, NA_H)).reshape(depth, 1, NA_W)
    w_out_b = w_out.astype(BF16)
    w_up_b = w_up.astype(BF16)
    w_down_b = w_down.astype(BF16)
    norm_mix3 = norm_mix.reshape(depth, 1, D_MODEL)
    norm_ffn3 = norm_ffn.reshape(depth, 1, D_MODEL)
    pool_scale3 = pool_scale.reshape(depth, 1, POOL_W)
    gdn_norm3 = gdn_norm.reshape(depth, 1, GDN_DH)
    s_zero = jnp.zeros((B, N_CHAIN, GDN_DH, GDN_DH), F32)
    na_bias = _na_bias_table(na_rpb)

    mods = _ada_call(cvec, w_ada, b_ada).reshape(depth, rows, 1, 6 * D_MODEL)

    for l in range(depth):
        mod = mods[l]
        with_ctx_out = l < depth - 1
        pv, gqkv, gz, gab, naq, nak, nav = _inproj_call(l, x, mod, None, norm_mix3, wts, qn_t, kn_t, gdn_conv,
                                                        rope_tables)
        cpv, cgqkv, cgz, cgab, cnaq, cnak, cnav = _inproj_call(l, ctx, mod, B, norm_mix3, wts, qn_t,
                                                               kn_t, gdn_conv, None)
        cprep = _gdn_prep_call(l, cgqkv, cgab, alog, dtb)
        co_f, co_b, s_ctx = _gdn_scan_call(cprep, s_zero)
        prep = _gdn_prep_call(l, gqkv, gab, alog, dtb)
        o_f, o_b, _ = _gdn_scan_call(prep, s_ctx)
        yc = _na_call(l, naq, nak, nav, cnak, cnav, na_bias)
        x = _outproj_call(l, x, mod, None, pv, o_f, o_b, gz, yc, gdn_norm3, pool_bd, pool_scale3, w_out_b, l > 0)
        x = _ffn_call(l, x, mod, None, norm_ffn3, w_up_b, ffn_conv, w_down_b)
        if with_ctx_out:
            cyc = _ctx_attn_call(cnaq, cnak, cnav)
            ctx = _outproj_call(l, ctx, mod, B, cpv, co_f, co_b, cgz, cyc, gdn_norm3, pool_bd, pool_scale3,
                                w_out_b, l > 0)
            ctx = _ffn_call(l, ctx, mod, B, norm_ffn3, w_up_b, ffn_conv, w_down_b)
    return x
```
